```python
import math
import jax, jax.numpy as jnp
from jax import lax
import numpy as np

D_MODEL = 1024
BATCH = 8
SEQ = 4096
DEPTH = 2

HEAD_DIM = 64
FOX_HEADS = (3 * D_MODEL) // (8 * HEAD_DIM)
FOX_WIDTH = FOX_HEADS * HEAD_DIM
DIFF_QK_DIM = 32
DIFF_V_DIM = 2 * DIFF_QK_DIM
DIFF_HEADS = D_MODEL // (4 * DIFF_V_DIM)
DIFF_WIDTH = DIFF_HEADS * DIFF_V_DIM
SGU_HEAD_DIM = 64
SGU_WIDTH = D_MODEL - FOX_WIDTH - DIFF_WIDTH
SGU_HEADS = SGU_WIDTH // SGU_HEAD_DIM
CHUNK = 128
Q_BLOCK = 128
IN_COLS = 3 * FOX_WIDTH + FOX_HEADS + 3 * DIFF_WIDTH + 2 * SGU_WIDTH
D_FF = 2816
ALPHA = (2 * DEPTH) ** 0.25
BETA = (8 * DEPTH) ** -0.25
LN_EPS = 1e-5
NEG_INF = -1e30

kernel_name = 'hybrid_fox_diff_sgu_macaron_deepnorm'


def _layer_norm(x, g, b):
    xf = x.astype(jnp.float32)
    mu = jnp.mean(xf, axis=-1, keepdims=True)
    var = jnp.mean(jnp.square(xf - mu), axis=-1, keepdims=True)
    return ((xf - mu) * lax.rsqrt(var + LN_EPS)).astype(x.dtype) * g + b


def _swiglu(x, w_gate, w_up, w_down):
    return (jax.nn.silu(x @ w_gate) * (x @ w_up)) @ w_down


def _to_blocks(a):
    b, s = a.shape[0], a.shape[1]
    a = a.reshape((b, s // Q_BLOCK, Q_BLOCK) + a.shape[2:])
    return jnp.moveaxis(a, 1, 0)


def _from_blocks(a):
    a = jnp.moveaxis(a, 0, 1)
    return a.reshape((a.shape[0], a.shape[1] * a.shape[2]) + a.shape[3:])


def _forgetting_attention(q, k, v, log_f):
    seq = q.shape[1]
    scale = HEAD_DIM ** -0.5
    c = jnp.cumsum(log_f, axis=1)
    c_k = jnp.transpose(c, (0, 2, 1))
    k_pos = jnp.arange(seq)

    def block(args):
        q_blk, c_blk, i = args
        q_pos = i * Q_BLOCK + jnp.arange(Q_BLOCK)
        s = jnp.einsum('bqhd,bkhd->bhqk', q_blk, k, preferred_element_type=jnp.float32) * scale
        s = s + jnp.transpose(c_blk, (0, 2, 1))[..., None] - c_k[:, :, None, :]
        s = jnp.where(k_pos[None, :] <= q_pos[:, None], s, NEG_INF)
        p = jax.nn.softmax(s, axis=-1).astype(v.dtype)
        return jnp.einsum('bhqk,bkhd->bqhd', p, v)

    nb = seq // Q_BLOCK
    out = lax.map(block, (_to_blocks(q), _to_blocks(c), jnp.arange(nb)))
    return _from_blocks(out)


def _diff_attention(q, k, v, lam, lam_init, norm_g):
    seq = q.shape[1]
    scale = DIFF_QK_DIM ** -0.5
    slopes = 2.0 ** (-8.0 * jnp.arange(1, DIFF_HEADS + 1, dtype=jnp.float32) / DIFF_HEADS)
    k_pos = jnp.arange(seq)

    def block(args):
        q_blk, i = args
        q_pos = i * Q_BLOCK + jnp.arange(Q_BLOCK)
        dist = (q_pos[:, None] - k_pos[None, :]).astype(jnp.float32)
        s = jnp.einsum('bqhmd,bkhmd->bhmqk', q_blk, k, preferred_element_type=jnp.float32) * scale
        s = s - slopes[None, :, None, None, None] * dist[None, None, None]
        s = jnp.where(dist >= 0, s, NEG_INF)
        p = jax.nn.softmax(s, axis=-1)
        a = (p[:, :, 0] - lam * p[:, :, 1]).astype(v.dtype)
        return jnp.einsum('bhqk,bkhe->bqhe', a, v)

    nb = seq // Q_BLOCK
    o = _from_blocks(lax.map(block, (_to_blocks(q), jnp.arange(nb))))
    of = o.astype(jnp.float32)
    o = (of * lax.rsqrt(jnp.mean(jnp.square(of), axis=-1, keepdims=True) + LN_EPS)).astype(v.dtype)
    o = o * norm_g.reshape(DIFF_HEADS, DIFF_V_DIM)
    return o * (1.0 - lam_init)


def _chunked_spatial_gating(z, ln_g, ln_b, w_s, b_s):
    u, g = jnp.split(z, 2, axis=-1)
    g = _layer_norm(g, ln_g, ln_b)
    b, s, _ = g.shape
    g = g.reshape(b, s // CHUNK, CHUNK, SGU_HEADS, SGU_HEAD_DIM)
    w = jnp.tril(w_s)
    mixed = jnp.einsum('hts,bcshd->bcthd', w, g) + jnp.transpose(b_s)[None, None, :, :, None]
    return u * mixed.reshape(b, s, SGU_WIDTH)


def _hybrid_mixer(x, w_in, f_bias, lq1, lk1, lq2, lk2, lam_init, diff_norm_g,
                  sgu_ln_g, sgu_ln_b, sgu_w_s, sgu_b_s, w_out):
    b, s, _ = x.shape
    proj = x @ w_in
    i1 = 3 * FOX_WIDTH
    i2 = i1 + FOX_HEADS
    i3 = i2 + 3 * DIFF_WIDTH
    fox_qkv, fox_f, diff_qkv, sgu_z = jnp.split(proj, [i1, i2, i3], axis=-1)
    fq, fk, fv = [t.reshape(b, s, FOX_HEADS, HEAD_DIM) for t in jnp.split(fox_qkv, 3, axis=-1)]
    log_f = jax.nn.log_sigmoid(fox_f.astype(jnp.float32) + f_bias.astype(jnp.float32))
    y_fox = _forgetting_attention(fq, fk, fv, log_f).reshape(b, s, FOX_WIDTH)
    dq, dk, dv = jnp.split(diff_qkv, 3, axis=-1)
    dq = dq.reshape(b, s, DIFF_HEADS, 2, DIFF_QK_DIM)
    dk = dk.reshape(b, s, DIFF_HEADS, 2, DIFF_QK_DIM)
    dv = dv.reshape(b, s, DIFF_HEADS, DIFF_V_DIM)
    lam = (jnp.exp(jnp.sum(lq1.astype(jnp.float32) * lk1.astype(jnp.float32)))
           - jnp.exp(jnp.sum(lq2.astype(jnp.float32) * lk2.astype(jnp.float32))) + lam_init)
    y_diff = _diff_attention(dq, dk, dv, lam, lam_init, diff_norm_g).reshape(b, s, DIFF_WIDTH)
    y_sgu = _chunked_spatial_gating(jax.nn.gelu(sgu_z), sgu_ln_g, sgu_ln_b, sgu_w_s, sgu_b_s)
    return jnp.concatenate([y_fox, y_diff, y_sgu], axis=-1) @ w_out


def setup_inputs(seed: int = 0) -> dict:
    key = jax.random.key(seed)
    ks = jax.random.split(key, 32)
    L, D, F = DEPTH, D_MODEL, D_FF

    def nrm(k, shape, scale):
        return jax.random.normal(k, shape, jnp.float32) * scale

    def gain(k, shape):
        return 1.0 + nrm(k, shape, 0.02)

    return {
        'x': nrm(ks[0], (BATCH, SEQ, D), 1.0),
        'ffn_a_w_gate': nrm(ks[1], (L, D, F), D ** -0.5),
        'ffn_a_w_up': nrm(ks[2], (L, D, F), D ** -0.5),
        'ffn_a_w_down': nrm(ks[3], (L, F, D), BETA * F ** -0.5),
        'norm_a_g': gain(ks[4], (L, D)),
        'norm_a_b': nrm(ks[5], (L, D), 0.02),
        'w_in': nrm(ks[6], (L, D, IN_COLS), D ** -0.5),
        'fox_f_bias': jax.random.uniform(ks[7], (L, FOX_HEADS), jnp.float32, 1.0, 4.0),
        'diff_lambda_q1': nrm(ks[8], (L, DIFF_QK_DIM), 0.1),
        'diff_lambda_k1': nrm(ks[9], (L, DIFF_QK_DIM), 0.1),
        'diff_lambda_q2': nrm(ks[10], (L, DIFF_QK_DIM), 0.1),
        'diff_lambda_k2': nrm(ks[11], (L, DIFF_QK_DIM), 0.1),
        'diff_norm_g': gain(ks[12], (L, DIFF_WIDTH)),
        'sgu_norm_g': gain(ks[13], (L, SGU_WIDTH)),
        'sgu_norm_b': nrm(ks[14], (L, SGU_WIDTH), 0.02),
        'sgu_w_s': nrm(ks[15], (L, SGU_HEADS, CHUNK, CHUNK), CHUNK ** -0.5),
        'sgu_b_s': 1.0 + nrm(ks[16], (L, SGU_HEADS, CHUNK), 0.1),
        'w_out': nrm(ks[17], (L, D, D), BETA * D ** -0.5),
        'norm_m_g': gain(ks[18], (L, D)),
        'norm_m_b': nrm(ks[19], (L, D), 0.02),
        'ffn_b_w_gate': nrm(ks[20], (L, D, F), D ** -0.5),
        'ffn_b_w_up': nrm(ks[21], (L, D, F), D ** -0.5),
        'ffn_b_w_down': nrm(ks[22], (L, F, D), BETA * F ** -0.5),
        'norm_b_g': gain(ks[23], (L, D)),
        'norm_b_b': nrm(ks[24], (L, D), 0.02),
    }


def reference(x, ffn_a_w_gate, ffn_a_w_up, ffn_a_w_down, norm_a_g, norm_a_b,
              w_in, fox_f_bias, diff_lambda_q1, diff_lambda_k1, diff_lambda_q2, diff_lambda_k2,
              diff_norm_g, sgu_norm_g, sgu_norm_b, sgu_w_s, sgu_b_s, w_out, norm_m_g, norm_m_b,
              ffn_b_w_gate, ffn_b_w_up, ffn_b_w_down, norm_b_g, norm_b_b):
    h = x
    for l in range(DEPTH):
        lam_init = 0.8 - 0.6 * math.exp(-0.3 * l)
        h = _layer_norm(ALPHA * h + 0.5 * _swiglu(h, ffn_a_w_gate[l], ffn_a_w_up[l], ffn_a_w_down[l]),
                        norm_a_g[l], norm_a_b[l])
        m = _hybrid_mixer(h, w_in[l], fox_f_bias[l], diff_lambda_q1[l], diff_lambda_k1[l],
                          diff_lambda_q2[l], diff_lambda_k2[l], lam_init, diff_norm_g[l],
                          sgu_norm_g[l], sgu_norm_b[l], sgu_w_s[l], sgu_b_s[l], w_out[l])
        h = _layer_norm(ALPHA * h + m, norm_m_g[l], norm_m_b[l])
        h = _layer_norm(ALPHA * h + 0.5 * _swiglu(h, ffn_b_w_gate[l], ffn_b_w_up[l], ffn_b_w_down[l]),
                        norm_b_g[l], norm_b_b[l])
    return h
```

```python
import functools
import math

import numpy as np
import jax
import jax.numpy as jnp
from jax import lax
from jax.experimental import pallas as pl
from jax.experimental.pallas import tpu as pltpu

F32 = jnp.float32
BF16 = jnp.bfloat16

HEAD_DIM = 64
DIFF_QK_DIM = 32
DIFF_V_DIM = 64
SGU_HEAD_DIM = 64
CHUNK = 128
LN_EPS = 1e-5
NEG_INF = -1e30

LANES = 128
VMEM_LIMIT = 56 * 1024 * 1024

FOX_EXT = HEAD_DIM
DIFF_EXT = DIFF_QK_DIM

TQ = 256
TK = 256
TM_FFN = 512
TM_IN = 512
TM_SGU = 512
TM_OUT = 512


def _nt_dot(a, b):
    return lax.dot_general(a, b, (((1,), (1,)), ((), ())), preferred_element_type=F32)


def _dot(a, b):
    return jnp.dot(a, b, preferred_element_type=F32)


def _layer_norm(y, g, b):
    mu = jnp.mean(y, axis=-1, keepdims=True)
    d = y - mu
    var = jnp.mean(d * d, axis=-1, keepdims=True)
    return d * lax.rsqrt(var + LN_EPS) * g + b


def _split3(x):
    hi = x.astype(BF16)
    r1 = x - hi.astype(F32)
    mid = r1.astype(BF16)
    r2 = r1 - mid.astype(F32)
    lo = r2.astype(BF16)
    return hi, mid, lo


def _split3_np(x):
    x = np.asarray(x, np.float32)
    hi = x.astype(BF16).astype(np.float32)
    r1 = x - hi
    mid = r1.astype(BF16).astype(np.float32)
    lo = (r1 - mid).astype(BF16).astype(np.float32)
    return hi, mid, lo


def _ffn_kernel(x_ref, wg_ref, wu_ref, wd_ref, g_ref, b_ref, o_ref, acc_ref, xb_ref, *, nf, alpha):
    j = pl.program_id(1)

    @pl.when(j == 0)
    def _():
        xb_ref[...] = x_ref[...].astype(BF16)
        acc_ref[...] = jnp.zeros_like(acc_ref)

    xb = xb_ref[...]
    gate = _dot(xb, wg_ref[...])
    up = _dot(xb, wu_ref[...])
    hidden = (gate * jax.nn.sigmoid(gate) * up).astype(BF16)
    acc_ref[...] += _dot(hidden, wd_ref[...])

    @pl.when(j == nf - 1)
    def _():
        y = alpha * x_ref[...] + 0.5 * acc_ref[...]
        o_ref[...] = _layer_norm(y, g_ref[...], b_ref[...])


def _ffn_ln(x2d, wg, wu, wd, g, b, *, alpha):
    t, d = x2d.shape
    f = wg.shape[1]
    tf = f // 2
    nf = f // tf
    tm = TM_FFN
    return pl.pallas_call(
        functools.partial(_ffn_kernel, nf=nf, alpha=alpha),
        grid=(t // tm, nf),
        in_specs=[
            pl.BlockSpec((tm, d), lambda i, j: (i, 0)),
            pl.BlockSpec((d, tf), lambda i, j: (0, j)),
            pl.BlockSpec((d, tf), lambda i, j: (0, j)),
            pl.BlockSpec((tf, d), lambda i, j: (j, 0)),
            pl.BlockSpec((1, d), lambda i, j: (0, 0)),
            pl.BlockSpec((1, d), lambda i, j: (0, 0)),
        ],
        out_specs=pl.BlockSpec((tm, d), lambda i, j: (i, 0)),
        out_shape=jax.ShapeDtypeStruct((t, d), F32),
        scratch_shapes=[pltpu.VMEM((tm, d), F32), pltpu.VMEM((tm, d), BF16)],
        compiler_params=pltpu.CompilerParams(
            dimension_semantics=("parallel", "arbitrary"), vmem_limit_bytes=VMEM_LIMIT),
        name="ffn_ln",
    )(x2d, wg, wu, wd, g, b)


def _mixer_in_kernel(h_ref, wqk_ref, wfvt_ref, wff_ref, fb_ref, wd_ref, wdvt_ref, tri_ref,
                     eqk_ref, ones_ref, ext2_ref,
                     qa_ref, ka_ref, vft_ref, q2_ref, k2_ref, vdt_ref, carry_ref,
                     *, fox_heads, diff_heads, diff_scale, fox_scale):
    s = pl.program_id(1)
    tm = h_ref.shape[0]
    hb = h_ref[...].astype(BF16)

    @pl.when(s == 0)
    def _():
        carry_ref[...] = jnp.zeros_like(carry_ref)

    fx = _dot(hb, wff_ref[...]) + fb_ref[...]
    log_f = -(jnp.maximum(-fx, 0.0) + jnp.log1p(jnp.exp(-jnp.abs(fx))))
    lane = lax.broadcasted_iota(jnp.int32, log_f.shape, 1)
    log_f = jnp.where(lane < fox_heads, log_f, 0.0)
    t3 = jnp.concatenate(_split3(log_f), axis=1)
    cs = _dot(tri_ref[...], t3)
    c = cs[:, :LANES] + cs[:, LANES:2 * LANES] + cs[:, 2 * LANES:] + carry_ref[...]
    carry_ref[...] = c[tm - 1:tm, :]
    c3 = jnp.concatenate(_split3(c), axis=1)
    ext = _dot(c3, eqk_ref[...]) + ones_ref[...]

    nq = fox_heads * LANES
    pqk = _dot(hb, wqk_ref[...])
    qa = (pqk[:, :nq] * fox_scale + ext[:, :nq]).astype(BF16)
    ka = (pqk[:, nq:] + ext[:, nq:]).astype(BF16)
    for h in range(fox_heads):
        qa_ref[h] = qa[:, h * LANES:(h + 1) * LANES]
        ka_ref[h] = ka[:, h * LANES:(h + 1) * LANES]

    vft = _nt_dot(wfvt_ref[...], hb).astype(BF16)
    for cblk in range(tm // TK):
        vft_ref[cblk] = vft[:, cblk * TK:(cblk + 1) * TK]

    nd = diff_heads * LANES
    pd = _dot(hb, wd_ref[...])
    ext2 = ext2_ref[...].astype(F32)
    q2 = (pd[:, :nd] * diff_scale + ext2[:, :nd]).astype(BF16)
    k2 = (pd[:, nd:] + ext2[:, nd:]).astype(BF16)
    for h in range(diff_heads):
        q2_ref[h] = q2[:, h * LANES:(h + 1) * LANES]
        k2_ref[h] = k2[:, h * LANES:(h + 1) * LANES]

    vdt = _nt_dot(wdvt_ref[...], hb).astype(BF16)
    for cblk in range(tm // TK):
        vdt_ref[cblk] = vdt[:, cblk * TK:(cblk + 1) * TK]


def _mixer_in(h, wqk, wfvt, wff, fb, wd, wdvt, tri, eqk, ones, ext2, *, fox_heads, diff_heads):
    b, s, d = h.shape
    tm = TM_IN
    nkv = s // TK
    fw = wfvt.shape[0]
    dw = wdvt.shape[0]
    const = lambda shape: pl.BlockSpec(shape, lambda bi, si: (0,) * len(shape))
    kern = functools.partial(
        _mixer_in_kernel, fox_heads=fox_heads, diff_heads=diff_heads,
        diff_scale=DIFF_QK_DIM ** -0.5, fox_scale=HEAD_DIM ** -0.5)
    return pl.pallas_call(
        kern,
        grid=(b, s // tm),
        in_specs=[
            pl.BlockSpec((None, tm, d), lambda bi, si: (bi, si, 0)),
            const(wqk.shape), const(wfvt.shape), const(wff.shape), const(fb.shape),
            const(wd.shape), const(wdvt.shape), const(tri.shape), const(eqk.shape),
            const(ones.shape),
            pl.BlockSpec((tm, ext2.shape[1]), lambda bi, si: (si, 0)),
        ],
        out_specs=[
            pl.BlockSpec((None, fox_heads, tm, LANES), lambda bi, si: (bi, 0, si, 0)),
            pl.BlockSpec((None, fox_heads, tm, LANES), lambda bi, si: (bi, 0, si, 0)),
            pl.BlockSpec((None, tm // TK, fw, TK), lambda bi, si: (bi, si, 0, 0)),
            pl.BlockSpec((None, diff_heads, tm, LANES), lambda bi, si: (bi, 0, si, 0)),
            pl.BlockSpec((None, diff_heads, tm, LANES), lambda bi, si: (bi, 0, si, 0)),
            pl.BlockSpec((None, tm // TK, dw, TK), lambda bi, si: (bi, si, 0, 0)),
        ],
        out_shape=[
            jax.ShapeDtypeStruct((b, fox_heads, s, LANES), BF16),
            jax.ShapeDtypeStruct((b, fox_heads, s, LANES), BF16),
            jax.ShapeDtypeStruct((b, nkv, fw, TK), BF16),
            jax.ShapeDtypeStruct((b, diff_heads, s, LANES), BF16),
            jax.ShapeDtypeStruct((b, diff_heads, s, LANES), BF16),
            jax.ShapeDtypeStruct((b, nkv, dw, TK), BF16),
        ],
        scratch_shapes=[pltpu.VMEM((1, LANES), F32)],
        compiler_params=pltpu.CompilerParams(
            dimension_semantics=("parallel", "arbitrary"), vmem_limit_bytes=VMEM_LIMIT),
        name="mixer_in",
    )(h, wqk, wfvt, wff, fb, wd, wdvt, tri, eqk, ones, ext2)


def _gelu_tanh(x):
    c = math.sqrt(2.0 / math.pi)
    return 0.5 * x * (1.0 + jnp.tanh(c * (x + 0.044715 * (x * x * x))))


def _sgu_kernel(h_ref, wu_ref, wg_ref, lg_ref, lb_ref, ws_ref, bs_ref, o_ref, *, heads):
    tm = h_ref.shape[0]
    hb = h_ref[...].astype(BF16)
    u = _gelu_tanh(_dot(hb, wu_ref[...]))
    g = _gelu_tanh(_dot(hb, wg_ref[...]))
    g = _layer_norm(g, lg_ref[...], lb_ref[...]).astype(BF16)
    row = lax.broadcasted_iota(jnp.int32, (CHUNK, CHUNK), 0)
    col = lax.broadcasted_iota(jnp.int32, (CHUNK, CHUNK), 1)
    causal = col <= row
    w = [jnp.where(causal, ws_ref[hh], 0.0).astype(BF16) for hh in range(heads)]
    lane = lax.broadcasted_iota(jnp.int32, (CHUNK, LANES), 1)
    low = lane < SGU_HEAD_DIM
    bias = bs_ref[...]
    for cblk in range(tm // CHUNK):
        rows = slice(cblk * CHUNK, (cblk + 1) * CHUNK)
        for p in range(heads // 2):
            cols = slice(p * LANES, (p + 1) * LANES)
            gp = g[rows, cols]
            zero = jnp.zeros_like(gp)
            mixed = (_dot(w[2 * p], jnp.where(low, gp, zero))
                     + _dot(w[2 * p + 1], jnp.where(low, zero, gp))
                     + bias[:, cols])
            o_ref[rows, cols] = (u[rows, cols] * mixed).astype(o_ref.dtype)


def _sgu(h2d, wu, wg, lg, lb, ws, bs_exp):
    t, d = h2d.shape
    width = wu.shape[1]
    heads = ws.shape[0]
    tm = TM_SGU
    const = lambda shape: pl.BlockSpec(shape, lambda i: (0,) * len(shape))
    return pl.pallas_call(
        functools.partial(_sgu_kernel, heads=heads),
        grid=(t // tm,),
        in_specs=[
            pl.BlockSpec((tm, d), lambda i: (i, 0)),
            const(wu.shape), const(wg.shape), const(lg.shape), const(lb.shape),
            const(ws.shape), const(bs_exp.shape),
        ],
        out_specs=pl.BlockSpec((tm, width), lambda i: (i, 0)),
        out_shape=jax.ShapeDtypeStruct((t, width), BF16),
        compiler_params=pltpu.CompilerParams(
            dimension_semantics=("parallel",), vmem_limit_bytes=VMEM_LIMIT),
        name="sgu",
    )(h2d, wu, wg, lg, lb, ws, bs_exp)


def _flash_step(q_op, k_blk, v_blk, carry, *, masked, cols_per_map):
    m, l, acc = carry
    st = _nt_dot(k_blk, q_op)
    if masked:
        key = lax.broadcasted_iota(jnp.int32, st.shape, 0)
        qry = lax.broadcasted_iota(jnp.int32, st.shape, 1)
        if cols_per_map != st.shape[1]:
            qry = jnp.where(qry >= cols_per_map, qry - cols_per_map, qry)
        st = jnp.where(key <= qry, st, NEG_INF)
    m_new = jnp.maximum(m, jnp.max(st, axis=0, keepdims=True))
    p = jnp.exp(st - m_new)
    scale = jnp.exp(m - m_new)
    l = scale * l + jnp.sum(p, axis=0, keepdims=True)
    acc = scale * acc + _dot(v_blk, p.astype(BF16))
    return m_new, l, acc


def _flash_sweep(q_ops, k_refs, v_refs, qi, *, cols_per_map):
    n = len(q_ops)
    nq = q_ops[0].shape[0]
    dv = v_refs[0].shape[1]
    init = tuple((jnp.full((1, nq), NEG_INF, F32), jnp.zeros((1, nq), F32),
                  jnp.zeros((dv, nq), F32)) for _ in range(n))

    def body(j, carries):
        start = pl.multiple_of(j * TK, TK)
        out = []
        for hh in range(n):
            k_blk = k_refs[hh][pl.ds(start, TK), :]
            out.append(_flash_step(q_ops[hh], k_blk, v_refs[hh][j], carries[hh],
                                   masked=False, cols_per_map=cols_per_map))
        return tuple(out)

    carries = lax.fori_loop(0, qi, body, init)
    start = pl.multiple_of(qi * TK, TK)
    final = []
    for hh in range(n):
        k_blk = k_refs[hh][pl.ds(start, TK), :]
        final.append(_flash_step(q_ops[hh], k_blk, v_refs[hh][qi], carries[hh],
                                 masked=True, cols_per_map=cols_per_map))
    return final


def _fox_kernel(q_ref, k_ref, v_ref, o_ref, *, hp):
    qi = pl.program_id(2)
    q_ops = [q_ref[hh] for hh in range(hp)]
    k_refs = [k_ref.at[hh] for hh in range(hp)]
    v_refs = [v_ref.at[:, pl.ds(hh * HEAD_DIM, HEAD_DIM), :] for hh in range(hp)]
    res = _flash_sweep(q_ops, k_refs, v_refs, qi, cols_per_map=TQ)
    for hh in range(hp):
        _, l, acc = res[hh]
        o_ref[pl.ds(hh * HEAD_DIM, HEAD_DIM), :] = (acc / l).astype(o_ref.dtype)


def _fox_attention(qa, ka, vft, *, hp=2):
    b, heads, s, _ = qa.shape
    nkv = vft.shape[1]
    return pl.pallas_call(
        functools.partial(_fox_kernel, hp=hp),
        grid=(b, heads // hp, s // TQ),
        in_specs=[
            pl.BlockSpec((None, hp, TQ, LANES), lambda bi, hi, qi: (bi, hi, qi, 0)),
            pl.BlockSpec((None, hp, s, LANES), lambda bi, hi, qi: (bi, hi, 0, 0)),
            pl.BlockSpec((None, nkv, hp * HEAD_DIM, TK), lambda bi, hi, qi: (bi, 0, hi, 0)),
        ],
        out_specs=pl.BlockSpec((None, hp * HEAD_DIM, TQ), lambda bi, hi, qi: (bi, hi, qi)),
        out_shape=jax.ShapeDtypeStruct((b, heads * HEAD_DIM, s), BF16),
        compiler_params=pltpu.CompilerParams(
            dimension_semantics=("parallel", "parallel", "arbitrary"),
            vmem_limit_bytes=VMEM_LIMIT),
        name="fox_attention",
    )(qa, ka, vft)


def _diff_kernel(q_ref, k_ref, v_ref, lq1_ref, lk1_ref, lq2_ref, lk2_ref, g_ref, o_ref,
                 *, hp, lam_init):
    qi = pl.program_id(2)
    lane = lax.broadcasted_iota(jnp.int32, (TQ, LANES), 1)
    first_map = lane < 2 * DIFF_QK_DIM
    q_ops = []
    for hh in range(hp):
        q2 = q_ref[hh]
        zero = jnp.zeros_like(q2)
        q_ops.append(jnp.concatenate(
            [jnp.where(first_map, q2, zero), jnp.where(first_map, zero, q2)], axis=0))
    k_refs = [k_ref.at[hh] for hh in range(hp)]
    v_refs = [v_ref.at[:, pl.ds(hh * DIFF_V_DIM, DIFF_V_DIM), :] for hh in range(hp)]
    res = _flash_sweep(q_ops, k_refs, v_refs, qi, cols_per_map=TQ)

    lam = (jnp.exp(jnp.sum(lq1_ref[...] * lk1_ref[...], axis=-1, keepdims=True))
           - jnp.exp(jnp.sum(lq2_ref[...] * lk2_ref[...], axis=-1, keepdims=True))
           + lam_init)
    for hh in range(hp):
        _, l, acc = res[hh]
        o = acc / l
        o = o[:, :TQ] - lam * o[:, TQ:]
        ms = jnp.mean(o * o, axis=0, keepdims=True)
        o = o * lax.rsqrt(ms + LN_EPS)
        o = o * g_ref[hh] * (1.0 - lam_init)
        o_ref[pl.ds(hh * DIFF_V_DIM, DIFF_V_DIM), :] = o.astype(o_ref.dtype)


def _diff_attention(q2, k2, vdt, lq1, lk1, lq2, lk2, gcol, *, lam_init, hp=2):
    b, heads, s, _ = q2.shape
    nkv = vdt.shape[1]
    vec = lambda a: pl.BlockSpec(a.shape, lambda bi, hi, qi: (0, 0))
    return pl.pallas_call(
        functools.partial(_diff_kernel, hp=hp, lam_init=lam_init),
        grid=(b, heads // hp, s // TQ),
        in_specs=[
            pl.BlockSpec((None, hp, TQ, LANES), lambda bi, hi, qi: (bi, hi, qi, 0)),
            pl.BlockSpec((None, hp, s, LANES), lambda bi, hi, qi: (bi, hi, 0, 0)),
            pl.BlockSpec((None, nkv, hp * DIFF_V_DIM, TK), lambda bi, hi, qi: (bi, 0, hi, 0)),
            vec(lq1), vec(lk1), vec(lq2), vec(lk2),
            pl.BlockSpec((hp, DIFF_V_DIM, 1), lambda bi, hi, qi: (hi, 0, 0)),
        ],
        out_specs=pl.BlockSpec((None, hp * DIFF_V_DIM, TQ), lambda bi, hi, qi: (bi, hi, qi)),
        out_shape=jax.ShapeDtypeStruct((b, heads * DIFF_V_DIM, s), BF16),
        compiler_params=pltpu.CompilerParams(
            dimension_semantics=("parallel", "parallel", "arbitrary"),
            vmem_limit_bytes=VMEM_LIMIT),
        name="diff_attention",
    )(q2, k2, vdt, lq1, lk1, lq2, lk2, gcol)


def _mixer_out_kernel(yf_ref, yd_ref, ys_ref, h_ref, w_ref, g_ref, b_ref, o_ref, *, alpha):
    yf = jnp.transpose(yf_ref[...].astype(F32)).astype(BF16)
    yd = jnp.transpose(yd_ref[...].astype(F32)).astype(BF16)
    y = jnp.concatenate([yf, yd, ys_ref[...]], axis=1)
    m = _dot(y, w_ref[...])
    o_ref[...] = _layer_norm(alpha * h_ref[...] + m, g_ref[...], b_ref[...])


def _mixer_out(yft, ydt, ys, h, w, g, b, *, alpha):
    bsz, s, d = h.shape
    fw, dw, sw = yft.shape[1], ydt.shape[1], ys.shape[2]
    tm = TM_OUT
    const = lambda shape: pl.BlockSpec(shape, lambda bi, si: (0,) * len(shape))
    return pl.pallas_call(
        functools.partial(_mixer_out_kernel, alpha=alpha),
        grid=(bsz, s // tm),
        in_specs=[
            pl.BlockSpec((None, fw, tm), lambda bi, si: (bi, 0, si)),
            pl.BlockSpec((None, dw, tm), lambda bi, si: (bi, 0, si)),
            pl.BlockSpec((None, tm, sw), lambda bi, si: (bi, si, 0)),
            pl.BlockSpec((None, tm, d), lambda bi, si: (bi, si, 0)),
            const(w.shape), const(g.shape), const(b.shape),
        ],
        out_specs=pl.BlockSpec((None, tm, d), lambda bi, si: (bi, si, 0)),
        out_shape=jax.ShapeDtypeStruct((bsz, s, d), F32),
        compiler_params=pltpu.CompilerParams(
            dimension_semantics=("parallel", "parallel"), vmem_limit_bytes=VMEM_LIMIT),
        name="mixer_out",
    )(yft, ydt, ys, h, w, g, b)


def _fox_bias_tables(fox_heads):
    nq = fox_heads * LANES
    eqk = np.zeros((3 * LANES, 2 * nq), np.float32)
    ones = np.zeros((1, 2 * nq), np.float32)
    for h in range(fox_heads):
        for t in range(3):
            ones[0, h * LANES + FOX_EXT + t] = 1.0
            eqk[t * LANES + h, h * LANES + FOX_EXT + 3 + t] = 1.0
            eqk[t * LANES + h, nq + h * LANES + FOX_EXT + t] = -1.0
            ones[0, nq + h * LANES + FOX_EXT + 3 + t] = 1.0
    return jnp.asarray(eqk, BF16), jnp.asarray(ones, F32)


def _alibi_tables(seq, diff_heads):
    nd = diff_heads * LANES
    ext = np.zeros((seq, 2 * nd), np.float32)
    pos = np.arange(seq, dtype=np.float32)
    slopes = np.float32(2.0) ** (np.float32(-8.0) * np.arange(1, diff_heads + 1, dtype=np.float32)
                                 / np.float32(diff_heads))
    for h in range(diff_heads):
        q_terms = _split3_np(-slopes[h] * pos)
        k_terms = _split3_np(slopes[h] * pos)
        for mp in range(2):
            base = h * LANES + mp * 2 * DIFF_QK_DIM + DIFF_EXT
            for t in range(3):
                ext[:, base + t] = 1.0
                ext[:, base + 3 + t] = q_terms[t]
                ext[:, nd + base + t] = k_terms[t]
                ext[:, nd + base + 3 + t] = 1.0
    return jnp.asarray(ext, BF16)


def _pad_heads(w, heads, dim):
    d = w.shape[0]
    return jnp.pad(w.reshape(d, heads, dim), ((0, 0), (0, 0), (0, LANES - dim))).reshape(d, heads * LANES)


def _pad_maps(w, heads):
    d = w.shape[0]
    w = w.reshape(d, heads, 2, DIFF_QK_DIM)
    w = jnp.pad(w, ((0, 0), (0, 0), (0, 0), (0, 2 * DIFF_QK_DIM - DIFF_QK_DIM)))
    return w.reshape(d, heads * LANES)


def kernel(x, ffn_a_w_gate, ffn_a_w_up, ffn_a_w_down, norm_a_g, norm_a_b, w_in, fox_f_bias, diff_lambda_q1, diff_lambda_k1, diff_lambda_q2, diff_lambda_k2, diff_norm_g, sgu_norm_g, sgu_norm_b, sgu_w_s, sgu_b_s, w_out, norm_m_g, norm_m_b, ffn_b_w_gate, ffn_b_w_up, ffn_b_w_down, norm_b_g, norm_b_b):
    bsz, seq, d = x.shape
    depth = w_in.shape[0]
    fox_heads = fox_f_bias.shape[1]
    fox_w = fox_heads * HEAD_DIM
    diff_w = diff_norm_g.shape[1]
    diff_heads = diff_w // DIFF_V_DIM
    sgu_w = sgu_norm_g.shape[1]
    sgu_heads = sgu_w_s.shape[1]
    alpha = (2 * depth) ** 0.25

    eqk, ones = _fox_bias_tables(fox_heads)
    ext2 = _alibi_tables(seq, diff_heads)
    tri = jnp.asarray(np.tril(np.ones((TM_IN, TM_IN), np.float32)), BF16)

    row = lambda v: v.reshape(1, -1)
    h = x
    for l in range(depth):
        lam_init = 0.8 - 0.6 * math.exp(-0.3 * l)
        h = _ffn_ln(h.reshape(bsz * seq, d), ffn_a_w_gate[l].astype(BF16), ffn_a_w_up[l].astype(BF16),
                    ffn_a_w_down[l].astype(BF16), row(norm_a_g[l]), row(norm_a_b[l]),
                    alpha=alpha).reshape(bsz, seq, d)

        wl = w_in[l]
        i1 = 3 * fox_w
        i2 = i1 + fox_heads
        i3 = i2 + 3 * diff_w
        wqk = jnp.concatenate([_pad_heads(wl[:, :fox_w], fox_heads, HEAD_DIM),
                               _pad_heads(wl[:, fox_w:2 * fox_w], fox_heads, HEAD_DIM)], axis=1).astype(BF16)
        wfvt = wl[:, 2 * fox_w:i1].T.astype(BF16)
        wff = jnp.pad(wl[:, i1:i2], ((0, 0), (0, LANES - fox_heads))).astype(BF16)
        fb = jnp.pad(fox_f_bias[l], (0, LANES - fox_heads)).reshape(1, LANES)
        wdqk = jnp.concatenate([_pad_maps(wl[:, i2:i2 + diff_w], diff_heads),
                                _pad_maps(wl[:, i2 + diff_w:i2 + 2 * diff_w], diff_heads)], axis=1).astype(BF16)
        wdvt = wl[:, i2 + 2 * diff_w:i3].T.astype(BF16)
        qa, ka, vft, q2, k2, vdt = _mixer_in(h, wqk, wfvt, wff, fb, wdqk, wdvt, tri, eqk, ones, ext2,
                                              fox_heads=fox_heads, diff_heads=diff_heads)

        bs_exp = jnp.repeat(jnp.transpose(sgu_b_s[l]), SGU_HEAD_DIM, axis=1)
        ys = _sgu(h.reshape(bsz * seq, d), wl[:, i3:i3 + sgu_w].astype(BF16), wl[:, i3 + sgu_w:].astype(BF16),
                  row(sgu_norm_g[l]), row(sgu_norm_b[l]), sgu_w_s[l], bs_exp).reshape(bsz, seq, sgu_w)

        yft = _fox_attention(qa, ka, vft)
        ydt = _diff_attention(q2, k2, vdt, row(diff_lambda_q1[l]), row(diff_lambda_k1[l]),
                              row(diff_lambda_q2[l]), row(diff_lambda_k2[l]),
                              diff_norm_g[l].reshape(diff_heads, DIFF_V_DIM, 1), lam_init=lam_init)

        h = _mixer_out(yft, ydt, ys, h, w_out[l].astype(BF16), row(norm_m_g[l]), row(norm_m_b[l]), alpha=alpha)

        h = _ffn_ln(h.reshape(bsz * seq, d), ffn_b_w_gate[l].astype(BF16), ffn_b_w_up[l].astype(BF16),
                    ffn_b_w_down[l].astype(BF16), row(norm_b_g[l]), row(norm_b_b[l]),
                    alpha=alpha).reshape(bsz, seq, d)
    return h
```

```python
import functools
import math

import numpy as np
import jax
import jax.numpy as jnp
from jax import lax
from jax.experimental import pallas as pl
from jax.experimental.pallas import tpu as pltpu

F32 = jnp.float32
BF16 = jnp.bfloat16

HEAD_DIM = 64
DIFF_QK_DIM = 32
DIFF_V_DIM = 64
SGU_HEAD_DIM = 64
CHUNK = 128
LN_EPS = 1e-5
NEG_INF = -1e30

LANES = 128
VMEM_LIMIT = 56 * 1024 * 1024

FOX_EXT = HEAD_DIM
DIFF_EXT = DIFF_QK_DIM

TQ = 512
TK = 256
TM_FFN = 512
TM_IN = 512
TM_SGU = 512
TM_OUT = 512


def _nt_dot(a, b):
    return lax.dot_general(a, b, (((1,), (1,)), ((), ())), preferred_element_type=F32)


def _dot(a, b):
    return jnp.dot(a, b, preferred_element_type=F32)


def _layer_norm(y, g, b):
    mu = jnp.mean(y, axis=-1, keepdims=True)
    d = y - mu
    var = jnp.mean(d * d, axis=-1, keepdims=True)
    return d * lax.rsqrt(var + LN_EPS) * g + b


def _split3(x):
    hi = x.astype(BF16)
    r1 = x - hi.astype(F32)
    mid = r1.astype(BF16)
    r2 = r1 - mid.astype(F32)
    lo = r2.astype(BF16)
    return hi, mid, lo


def _split3_np(x):
    x = np.asarray(x, np.float32)
    hi = x.astype(BF16).astype(np.float32)
    r1 = x - hi
    mid = r1.astype(BF16).astype(np.float32)
    lo = (r1 - mid).astype(BF16).astype(np.float32)
    return hi, mid, lo


def _ffn_kernel(x_ref, wg_ref, wu_ref, wd_ref, g_ref, b_ref, o_ref, acc_ref, xb_ref, *, nf, alpha):
    j = pl.program_id(1)

    @pl.when(j == 0)
    def _():
        xb_ref[...] = x_ref[...].astype(BF16)
        acc_ref[...] = jnp.zeros_like(acc_ref)

    xb = xb_ref[...]
    gate = _dot(xb, wg_ref[...])
    up = _dot(xb, wu_ref[...])
    hidden = (gate * jax.nn.sigmoid(gate) * up).astype(BF16)
    acc_ref[...] += _dot(hidden, wd_ref[...])

    @pl.when(j == nf - 1)
    def _():
        y = alpha * x_ref[...] + 0.5 * acc_ref[...]
        o_ref[...] = _layer_norm(y, g_ref[...], b_ref[...])


def _ffn_ln(x2d, wg, wu, wd, g, b, *, alpha):
    t, d = x2d.shape
    f = wg.shape[1]
    tf = f // 2
    nf = f // tf
    tm = TM_FFN
    return pl.pallas_call(
        functools.partial(_ffn_kernel, nf=nf, alpha=alpha),
        grid=(t // tm, nf),
        in_specs=[
            pl.BlockSpec((tm, d), lambda i, j: (i, 0)),
            pl.BlockSpec((d, tf), lambda i, j: (0, j)),
            pl.BlockSpec((d, tf), lambda i, j: (0, j)),
            pl.BlockSpec((tf, d), lambda i, j: (j, 0)),
            pl.BlockSpec((1, d), lambda i, j: (0, 0)),
            pl.BlockSpec((1, d), lambda i, j: (0, 0)),
        ],
        out_specs=pl.BlockSpec((tm, d), lambda i, j: (i, 0)),
        out_shape=jax.ShapeDtypeStruct((t, d), F32),
        scratch_shapes=[pltpu.VMEM((tm, d), F32), pltpu.VMEM((tm, d), BF16)],
        compiler_params=pltpu.CompilerParams(
            dimension_semantics=("parallel", "arbitrary"), vmem_limit_bytes=VMEM_LIMIT),
        name="ffn_ln",
    )(x2d, wg, wu, wd, g, b)


def _mixer_in_kernel(h_ref, wqk_ref, wfvt_ref, wff_ref, fb_ref, wd_ref, wdvt_ref, tri_ref,
                     eqk_ref, ones_ref, ext2_ref,
                     qa_ref, ka_ref, vft_ref, q2_ref, k2_ref, vdt_ref, carry_ref,
                     *, fox_heads, diff_heads, diff_scale, fox_scale):
    s = pl.program_id(1)
    tm = h_ref.shape[0]
    hb = h_ref[...].astype(BF16)

    @pl.when(s == 0)
    def _():
        carry_ref[...] = jnp.zeros_like(carry_ref)

    fx = _dot(hb, wff_ref[...]) + fb_ref[...]
    log_f = -(jnp.maximum(-fx, 0.0) + jnp.log1p(jnp.exp(-jnp.abs(fx))))
    lane = lax.broadcasted_iota(jnp.int32, log_f.shape, 1)
    log_f = jnp.where(lane < fox_heads, log_f, 0.0)
    t3 = jnp.concatenate(_split3(log_f), axis=1)
    cs = _dot(tri_ref[...], t3)
    c = cs[:, :LANES] + cs[:, LANES:2 * LANES] + cs[:, 2 * LANES:] + carry_ref[...]
    carry_ref[...] = c[tm - 1:tm, :]
    c3 = jnp.concatenate(_split3(c), axis=1)
    ext = _dot(c3, eqk_ref[...]) + ones_ref[...]

    nq = fox_heads * LANES
    pqk = _dot(hb, wqk_ref[...])
    qa = (pqk[:, :nq] * fox_scale + ext[:, :nq]).astype(BF16)
    ka = (pqk[:, nq:] + ext[:, nq:]).astype(BF16)
    for h in range(fox_heads):
        qa_ref[h] = qa[:, h * LANES:(h + 1) * LANES]
        ka_ref[h] = ka[:, h * LANES:(h + 1) * LANES]

    vft = _nt_dot(wfvt_ref[...], hb).astype(BF16)
    for cblk in range(tm // TK):
        vft_ref[cblk] = vft[:, cblk * TK:(cblk + 1) * TK]

    nd = diff_heads * LANES
    pd = _dot(hb, wd_ref[...])
    ext2 = ext2_ref[...].astype(F32)
    q2 = (pd[:, :nd] * diff_scale + ext2[:, :nd]).astype(BF16)
    k2 = (pd[:, nd:] + ext2[:, nd:]).astype(BF16)
    for h in range(diff_heads):
        q2_ref[h] = q2[:, h * LANES:(h + 1) * LANES]
        k2_ref[h] = k2[:, h * LANES:(h + 1) * LANES]

    vdt = _nt_dot(wdvt_ref[...], hb).astype(BF16)
    for cblk in range(tm // TK):
        vdt_ref[cblk] = vdt[:, cblk * TK:(cblk + 1) * TK]


def _mixer_in(h, wqk, wfvt, wff, fb, wd, wdvt, tri, eqk, ones, ext2, *, fox_heads, diff_heads):
    b, s, d = h.shape
    tm = TM_IN
    nkv = s // TK
    fw = wfvt.shape[0]
    dw = wdvt.shape[0]
    const = lambda shape: pl.BlockSpec(shape, lambda bi, si: (0,) * len(shape))
    kern = functools.partial(
        _mixer_in_kernel, fox_heads=fox_heads, diff_heads=diff_heads,
        diff_scale=DIFF_QK_DIM ** -0.5, fox_scale=HEAD_DIM ** -0.5)
    return pl.pallas_call(
        kern,
        grid=(b, s // tm),
        in_specs=[
            pl.BlockSpec((None, tm, d), lambda bi, si: (bi, si, 0)),
            const(wqk.shape), const(wfvt.shape), const(wff.shape), const(fb.shape),
            const(wd.shape), const(wdvt.shape), const(tri.shape), const(eqk.shape),
            const(ones.shape),
            pl.BlockSpec((tm, ext2.shape[1]), lambda bi, si: (si, 0)),
        ],
        out_specs=[
            pl.BlockSpec((None, fox_heads, tm, LANES), lambda bi, si: (bi, 0, si, 0)),
            pl.BlockSpec((None, fox_heads, tm, LANES), lambda bi, si: (bi, 0, si, 0)),
            pl.BlockSpec((None, tm // TK, fw, TK), lambda bi, si: (bi, si, 0, 0)),
            pl.BlockSpec((None, diff_heads, tm, LANES), lambda bi, si: (bi, 0, si, 0)),
            pl.BlockSpec((None, diff_heads, tm, LANES), lambda bi, si: (bi, 0, si, 0)),
            pl.BlockSpec((None, tm // TK, dw, TK), lambda bi, si: (bi, si, 0, 0)),
        ],
        out_shape=[
            jax.ShapeDtypeStruct((b, fox_heads, s, LANES), BF16),
            jax.ShapeDtypeStruct((b, fox_heads, s, LANES), BF16),
            jax.ShapeDtypeStruct((b, nkv, fw, TK), BF16),
            jax.ShapeDtypeStruct((b, diff_heads, s, LANES), BF16),
            jax.ShapeDtypeStruct((b, diff_heads, s, LANES), BF16),
            jax.ShapeDtypeStruct((b, nkv, dw, TK), BF16),
        ],
        scratch_shapes=[pltpu.VMEM((1, LANES), F32)],
        compiler_params=pltpu.CompilerParams(
            dimension_semantics=("parallel", "arbitrary"), vmem_limit_bytes=VMEM_LIMIT),
        name="mixer_in",
    )(h, wqk, wfvt, wff, fb, wd, wdvt, tri, eqk, ones, ext2)


def _gelu_tanh(x):
    c = math.sqrt(2.0 / math.pi)
    return 0.5 * x * (1.0 + jnp.tanh(c * (x + 0.044715 * (x * x * x))))


def _sgu_kernel(h_ref, wu_ref, wg_ref, lg_ref, lb_ref, ws_ref, bs_ref, o_ref, *, heads):
    tm = h_ref.shape[0]
    hb = h_ref[...].astype(BF16)
    u = _gelu_tanh(_dot(hb, wu_ref[...]))
    g = _gelu_tanh(_dot(hb, wg_ref[...]))
    g = _layer_norm(g, lg_ref[...], lb_ref[...]).astype(BF16)
    row = lax.broadcasted_iota(jnp.int32, (CHUNK, CHUNK), 0)
    col = lax.broadcasted_iota(jnp.int32, (CHUNK, CHUNK), 1)
    causal = col <= row
    w = [jnp.where(causal, ws_ref[hh], 0.0).astype(BF16) for hh in range(heads)]
    lane = lax.broadcasted_iota(jnp.int32, (CHUNK, LANES), 1)
    low = lane < SGU_HEAD_DIM
    bias = bs_ref[...]
    for cblk in range(tm // CHUNK):
        rows = slice(cblk * CHUNK, (cblk + 1) * CHUNK)
        for p in range(heads // 2):
            cols = slice(p * LANES, (p + 1) * LANES)
            gp = g[rows, cols]
            zero = jnp.zeros_like(gp)
            mixed = (_dot(w[2 * p], jnp.where(low, gp, zero))
                     + _dot(w[2 * p + 1], jnp.where(low, zero, gp))
                     + bias[:, cols])
            o_ref[rows, cols] = (u[rows, cols] * mixed).astype(o_ref.dtype)


def _sgu(h2d, wu, wg, lg, lb, ws, bs_exp):
    t, d = h2d.shape
    width = wu.shape[1]
    heads = ws.shape[0]
    tm = TM_SGU
    const = lambda shape: pl.BlockSpec(shape, lambda i: (0,) * len(shape))
    return pl.pallas_call(
        functools.partial(_sgu_kernel, heads=heads),
        grid=(t // tm,),
        in_specs=[
            pl.BlockSpec((tm, d), lambda i: (i, 0)),
            const(wu.shape), const(wg.shape), const(lg.shape), const(lb.shape),
            const(ws.shape), const(bs_exp.shape),
        ],
        out_specs=pl.BlockSpec((tm, width), lambda i: (i, 0)),
        out_shape=jax.ShapeDtypeStruct((t, width), BF16),
        compiler_params=pltpu.CompilerParams(
            dimension_semantics=("parallel",), vmem_limit_bytes=VMEM_LIMIT),
        name="sgu",
    )(h2d, wu, wg, lg, lb, ws, bs_exp)


def _softmax_block(st, m, l, *, key_offset):
    if key_offset is not None:
        key = lax.broadcasted_iota(jnp.int32, st.shape, 0) + key_offset
        qry = lax.broadcasted_iota(jnp.int32, st.shape, 1)
        st = jnp.where(key <= qry, st, NEG_INF)
    m_new = jnp.maximum(m, jnp.max(st, axis=0, keepdims=True))
    p = jnp.exp(st - m_new)
    scale = jnp.exp(m - m_new)
    l = scale * l + jnp.sum(p, axis=0, keepdims=True)
    return m_new, l, p.astype(BF16), scale


def _flash_sweep(qts, k_refs, v_refs, qi, s_ref, p_ref, m_ref, l_ref, acc_ref):
    n = len(qts)
    assert TQ == 2 * TK

    def scores(i, j, slot):
        start = pl.multiple_of(j * TK, TK)
        s_ref[i, slot] = _dot(k_refs[i][pl.ds(start, TK), :], qts[i])

    def step(i, j, slot, *, key_offset, issue_next):
        if issue_next:
            scores(i, j + 1, 1 - slot)
        pv = _dot(v_refs[i][jnp.maximum(j - 1, 0)], p_ref[i, 1 - slot])
        m, l, p, scale = _softmax_block(s_ref[i, slot], m_ref[i], l_ref[i], key_offset=key_offset)
        m_ref[i] = m
        l_ref[i] = l
        acc_ref[i] = (acc_ref[i] + pv) * scale
        return p

    for i in range(n):
        scores(i, 0, 0)
        p_ref[i, 1] = jnp.zeros(p_ref.shape[2:], BF16)
        m_ref[i] = jnp.full(m_ref.shape[1:], NEG_INF, F32)
        l_ref[i] = jnp.zeros(l_ref.shape[1:], F32)
        acc_ref[i] = jnp.zeros(acc_ref.shape[1:], F32)

    @pl.loop(0, qi)
    def _(u):
        for i in range(n):
            p_ref[i, 0] = step(i, 2 * u, 0, key_offset=None, issue_next=True)
        for i in range(n):
            p_ref[i, 1] = step(i, 2 * u + 1, 1, key_offset=None, issue_next=True)

    for i in range(n):
        p_ref[i, 0] = step(i, 2 * qi, 0, key_offset=0, issue_next=True)
    for i in range(n):
        p = step(i, 2 * qi + 1, 1, key_offset=TK, issue_next=False)
        acc_ref[i] = acc_ref[i] + _dot(v_refs[i][2 * qi + 1], p)


def _sweep_scratch(n, dv):
    return [pltpu.VMEM((n, 2, TK, TQ), F32), pltpu.VMEM((n, 2, TK, TQ), BF16),
            pltpu.VMEM((n, 1, TQ), F32), pltpu.VMEM((n, 1, TQ), F32), pltpu.VMEM((n, dv, TQ), F32)]


def _transposed(q):
    return jnp.transpose(q.astype(F32)).astype(BF16)


def _fox_kernel(q_ref, k_ref, v_ref, o_ref, s_ref, p_ref, m_ref, l_ref, acc_ref, *, hp):
    qi = pl.program_id(2)
    qts = [_transposed(q_ref[hh]) for hh in range(hp)]
    k_refs = [k_ref.at[hh] for hh in range(hp)]
    v_refs = [v_ref.at[:, pl.ds(hh * HEAD_DIM, HEAD_DIM), :] for hh in range(hp)]
    _flash_sweep(qts, k_refs, v_refs, qi, s_ref, p_ref, m_ref, l_ref, acc_ref)
    for hh in range(hp):
        o_ref[pl.ds(hh * HEAD_DIM, HEAD_DIM), :] = (acc_ref[hh] / l_ref[hh]).astype(o_ref.dtype)


def _fox_attention(qa, ka, vft, *, hp=2):
    b, heads, s, _ = qa.shape
    nkv = vft.shape[1]
    return pl.pallas_call(
        functools.partial(_fox_kernel, hp=hp),
        grid=(b, heads // hp, s // TQ),
        in_specs=[
            pl.BlockSpec((None, hp, TQ, LANES), lambda bi, hi, qi: (bi, hi, qi, 0)),
            pl.BlockSpec((None, hp, s, LANES), lambda bi, hi, qi: (bi, hi, 0, 0)),
            pl.BlockSpec((None, nkv, hp * HEAD_DIM, TK), lambda bi, hi, qi: (bi, 0, hi, 0)),
        ],
        out_specs=pl.BlockSpec((None, hp * HEAD_DIM, TQ), lambda bi, hi, qi: (bi, hi, qi)),
        out_shape=jax.ShapeDtypeStruct((b, heads * HEAD_DIM, s), BF16),
        scratch_shapes=_sweep_scratch(hp, HEAD_DIM),
        compiler_params=pltpu.CompilerParams(
            dimension_semantics=("parallel", "parallel", "arbitrary"),
            vmem_limit_bytes=VMEM_LIMIT),
        name="fox_attention",
    )(qa, ka, vft)


def _diff_kernel(q_ref, k_ref, v_ref, lq1_ref, lk1_ref, lq2_ref, lk2_ref, g_ref, o_ref,
                 s_ref, p_ref, m_ref, l_ref, acc_ref, *, lam_init):
    qi = pl.program_id(2)
    q2 = q_ref[...]
    lane = lax.broadcasted_iota(jnp.int32, q2.shape, 1)
    first_map = lane < 2 * DIFF_QK_DIM
    zero = jnp.zeros_like(q2)
    qts = [_transposed(jnp.where(first_map, q2, zero)), _transposed(jnp.where(first_map, zero, q2))]
    _flash_sweep(qts, [k_ref, k_ref], [v_ref, v_ref], qi, s_ref, p_ref, m_ref, l_ref, acc_ref)

    lam = (jnp.exp(jnp.sum(lq1_ref[...] * lk1_ref[...], axis=-1, keepdims=True))
           - jnp.exp(jnp.sum(lq2_ref[...] * lk2_ref[...], axis=-1, keepdims=True))
           + lam_init)
    o = acc_ref[0] / l_ref[0] - lam * (acc_ref[1] / l_ref[1])
    ms = jnp.mean(o * o, axis=0, keepdims=True)
    o = o * lax.rsqrt(ms + LN_EPS)
    o = o * g_ref[...] * (1.0 - lam_init)
    o_ref[...] = o.astype(o_ref.dtype)


def _diff_attention(q2, k2, vdt, lq1, lk1, lq2, lk2, gcol, *, lam_init):
    b, heads, s, _ = q2.shape
    nkv = vdt.shape[1]
    vec = lambda a: pl.BlockSpec(a.shape, lambda bi, hi, qi: (0, 0))
    return pl.pallas_call(
        functools.partial(_diff_kernel, lam_init=lam_init),
        grid=(b, heads, s // TQ),
        in_specs=[
            pl.BlockSpec((None, None, TQ, LANES), lambda bi, hi, qi: (bi, hi, qi, 0)),
            pl.BlockSpec((None, None, s, LANES), lambda bi, hi, qi: (bi, hi, 0, 0)),
            pl.BlockSpec((None, nkv, DIFF_V_DIM, TK), lambda bi, hi, qi: (bi, 0, hi, 0)),
            vec(lq1), vec(lk1), vec(lq2), vec(lk2),
            pl.BlockSpec((None, DIFF_V_DIM, 1), lambda bi, hi, qi: (hi, 0, 0)),
        ],
        out_specs=pl.BlockSpec((None, DIFF_V_DIM, TQ), lambda bi, hi, qi: (bi, hi, qi)),
        out_shape=jax.ShapeDtypeStruct((b, heads * DIFF_V_DIM, s), BF16),
        scratch_shapes=_sweep_scratch(2, DIFF_V_DIM),
        compiler_params=pltpu.CompilerParams(
            dimension_semantics=("parallel", "parallel", "arbitrary"),
            vmem_limit_bytes=VMEM_LIMIT),
        name="diff_attention",
    )(q2, k2, vdt, lq1, lk1, lq2, lk2, gcol)


def _mixer_out_kernel(yf_ref, yd_ref, ys_ref, h_ref, w_ref, g_ref, b_ref, o_ref, *, alpha):
    yf = jnp.transpose(yf_ref[...].astype(F32)).astype(BF16)
    yd = jnp.transpose(yd_ref[...].astype(F32)).astype(BF16)
    y = jnp.concatenate([yf, yd, ys_ref[...]], axis=1)
    m = _dot(y, w_ref[...])
    o_ref[...] = _layer_norm(alpha * h_ref[...] + m, g_ref[...], b_ref[...])


def _mixer_out(yft, ydt, ys, h, w, g, b, *, alpha):
    bsz, s, d = h.shape
    fw, dw, sw = yft.shape[1], ydt.shape[1], ys.shape[2]
    tm = TM_OUT
    const = lambda shape: pl.BlockSpec(shape, lambda bi, si: (0,) * len(shape))
    return pl.pallas_call(
        functools.partial(_mixer_out_kernel, alpha=alpha),
        grid=(bsz, s // tm),
        in_specs=[
            pl.BlockSpec((None, fw, tm), lambda bi, si: (bi, 0, si)),
            pl.BlockSpec((None, dw, tm), lambda bi, si: (bi, 0, si)),
            pl.BlockSpec((None, tm, sw), lambda bi, si: (bi, si, 0)),
            pl.BlockSpec((None, tm, d), lambda bi, si: (bi, si, 0)),
            const(w.shape), const(g.shape), const(b.shape),
        ],
        out_specs=pl.BlockSpec((None, tm, d), lambda bi, si: (bi, si, 0)),
        out_shape=jax.ShapeDtypeStruct((bsz, s, d), F32),
        compiler_params=pltpu.CompilerParams(
            dimension_semantics=("parallel", "parallel"), vmem_limit_bytes=VMEM_LIMIT),
        name="mixer_out",
    )(yft, ydt, ys, h, w, g, b)


def _fox_bias_tables(fox_heads):
    nq = fox_heads * LANES
    eqk = np.zeros((3 * LANES, 2 * nq), np.float32)
    ones = np.zeros((1, 2 * nq), np.float32)
    for h in range(fox_heads):
        for t in range(3):
            ones[0, h * LANES + FOX_EXT + t] = 1.0
            eqk[t * LANES + h, h * LANES + FOX_EXT + 3 + t] = 1.0
            eqk[t * LANES + h, nq + h * LANES + FOX_EXT + t] = -1.0
            ones[0, nq + h * LANES + FOX_EXT + 3 + t] = 1.0
    return jnp.asarray(eqk, BF16), jnp.asarray(ones, F32)


def _alibi_tables(seq, diff_heads):
    nd = diff_heads * LANES
    ext = np.zeros((seq, 2 * nd), np.float32)
    pos = np.arange(seq, dtype=np.float32)
    slopes = np.float32(2.0) ** (np.float32(-8.0) * np.arange(1, diff_heads + 1, dtype=np.float32)
                                 / np.float32(diff_heads))
    for h in range(diff_heads):
        q_terms = _split3_np(-slopes[h] * pos)
        k_terms = _split3_np(slopes[h] * pos)
        for mp in range(2):
            base = h * LANES + mp * 2 * DIFF_QK_DIM + DIFF_EXT
            for t in range(3):
                ext[:, base + t] = 1.0
                ext[:, base + 3 + t] = q_terms[t]
                ext[:, nd + base + t] = k_terms[t]
                ext[:, nd + base + 3 + t] = 1.0
    return jnp.asarray(ext, BF16)


def _pad_heads(w, heads, dim):
    d = w.shape[0]
    return jnp.pad(w.reshape(d, heads, dim), ((0, 0), (0, 0), (0, LANES - dim))).reshape(d, heads * LANES)


def _pad_maps(w, heads):
    d = w.shape[0]
    w = w.reshape(d, heads, 2, DIFF_QK_DIM)
    w = jnp.pad(w, ((0, 0), (0, 0), (0, 0), (0, 2 * DIFF_QK_DIM - DIFF_QK_DIM)))
    return w.reshape(d, heads * LANES)


def kernel(x, ffn_a_w_gate, ffn_a_w_up, ffn_a_w_down, norm_a_g, norm_a_b, w_in, fox_f_bias, diff_lambda_q1, diff_lambda_k1, diff_lambda_q2, diff_lambda_k2, diff_norm_g, sgu_norm_g, sgu_norm_b, sgu_w_s, sgu_b_s, w_out, norm_m_g, norm_m_b, ffn_b_w_gate, ffn_b_w_up, ffn_b_w_down, norm_b_g, norm_b_b):
    bsz, seq, d = x.shape
    depth = w_in.shape[0]
    fox_heads = fox_f_bias.shape[1]
    fox_w = fox_heads * HEAD_DIM
    diff_w = diff_norm_g.shape[1]
    diff_heads = diff_w // DIFF_V_DIM
    sgu_w = sgu_norm_g.shape[1]
    sgu_heads = sgu_w_s.shape[1]
    alpha = (2 * depth) ** 0.25

    eqk, ones = _fox_bias_tables(fox_heads)
    ext2 = _alibi_tables(seq, diff_heads)
    tri = jnp.asarray(np.tril(np.ones((TM_IN, TM_IN), np.float32)), BF16)

    row = lambda v: v.reshape(1, -1)
    h = x
    for l in range(depth):
        lam_init = 0.8 - 0.6 * math.exp(-0.3 * l)
        h = _ffn_ln(h.reshape(bsz * seq, d), ffn_a_w_gate[l].astype(BF16), ffn_a_w_up[l].astype(BF16),
                    ffn_a_w_down[l].astype(BF16), row(norm_a_g[l]), row(norm_a_b[l]),
                    alpha=alpha).reshape(bsz, seq, d)

        wl = w_in[l]
        i1 = 3 * fox_w
        i2 = i1 + fox_heads
        i3 = i2 + 3 * diff_w
        wqk = jnp.concatenate([_pad_heads(wl[:, :fox_w], fox_heads, HEAD_DIM),
                               _pad_heads(wl[:, fox_w:2 * fox_w], fox_heads, HEAD_DIM)], axis=1).astype(BF16)
        wfvt = wl[:, 2 * fox_w:i1].T.astype(BF16)
        wff = jnp.pad(wl[:, i1:i2], ((0, 0), (0, LANES - fox_heads))).astype(BF16)
        fb = jnp.pad(fox_f_bias[l], (0, LANES - fox_heads)).reshape(1, LANES)
        wdqk = jnp.concatenate([_pad_maps(wl[:, i2:i2 + diff_w], diff_heads),
                                _pad_maps(wl[:, i2 + diff_w:i2 + 2 * diff_w], diff_heads)], axis=1).astype(BF16)
        wdvt = wl[:, i2 + 2 * diff_w:i3].T.astype(BF16)
        qa, ka, vft, q2, k2, vdt = _mixer_in(h, wqk, wfvt, wff, fb, wdqk, wdvt, tri, eqk, ones, ext2,
                                              fox_heads=fox_heads, diff_heads=diff_heads)

        bs_exp = jnp.repeat(jnp.transpose(sgu_b_s[l]), SGU_HEAD_DIM, axis=1)
        ys = _sgu(h.reshape(bsz * seq, d), wl[:, i3:i3 + sgu_w].astype(BF16), wl[:, i3 + sgu_w:].astype(BF16),
                  row(sgu_norm_g[l]), row(sgu_norm_b[l]), sgu_w_s[l], bs_exp).reshape(bsz, seq, sgu_w)

        yft = _fox_attention(qa, ka, vft)
        ydt = _diff_attention(q2, k2, vdt, row(diff_lambda_q1[l]), row(diff_lambda_k1[l]),
                              row(diff_lambda_q2[l]), row(diff_lambda_k2[l]),
                              diff_norm_g[l].reshape(diff_heads, DIFF_V_DIM, 1), lam_init=lam_init)

        h = _mixer_out(yft, ydt, ys, h, w_out[l].astype(BF16), row(norm_m_g[l]), row(norm_m_b[l]), alpha=alpha)

        h = _ffn_ln(h.reshape(bsz * seq, d), ffn_b_w_gate[l].astype(BF16), ffn_b_w_up[l].astype(BF16),
                    ffn_b_w_down[l].astype(BF16), row(norm_b_g[l]), row(norm_b_b[l]),
                    alpha=alpha).reshape(bsz, seq, d)
    return h
```

```python
import functools
import math

import numpy as np
import jax
import jax.numpy as jnp
from jax import lax
from jax.experimental import pallas as pl
from jax.experimental.pallas import tpu as pltpu

F32 = jnp.float32
BF16 = jnp.bfloat16

HEAD_DIM = 64
DIFF_QK_DIM = 32
DIFF_V_DIM = 64
SGU_HEAD_DIM = 64
CHUNK = 128
LN_EPS = 1e-5
NEG_INF = -1e30
LOG2E = math.log2(math.e)
DV_AUG = HEAD_DIM + 16

LANES = 128
VMEM_LIMIT = 56 * 1024 * 1024

FOX_EXT = HEAD_DIM
DIFF_EXT = DIFF_QK_DIM

TQ = 512
TK = 256
TM_FFN = 512
TM_IN = 512
TM_SGU = 512
TM_OUT = 512


def _nt_dot(a, b):
    return lax.dot_general(a, b, (((1,), (1,)), ((), ())), preferred_element_type=F32)


def _dot(a, b):
    return jnp.dot(a, b, preferred_element_type=F32)


def _layer_norm(y, g, b):
    mu = jnp.mean(y, axis=-1, keepdims=True)
    d = y - mu
    var = jnp.mean(d * d, axis=-1, keepdims=True)
    return d * lax.rsqrt(var + LN_EPS) * g + b


def _split3(x):
    hi = x.astype(BF16)
    r1 = x - hi.astype(F32)
    mid = r1.astype(BF16)
    r2 = r1 - mid.astype(F32)
    lo = r2.astype(BF16)
    return hi, mid, lo


def _split3_np(x):
    x = np.asarray(x, np.float32)
    hi = x.astype(BF16).astype(np.float32)
    r1 = x - hi
    mid = r1.astype(BF16).astype(np.float32)
    lo = (r1 - mid).astype(BF16).astype(np.float32)
    return hi, mid, lo


def _ffn_kernel(x_ref, wg_ref, wu_ref, wd_ref, g_ref, b_ref, o_ref, *, alpha):
    x = x_ref[...]
    xb = x.astype(BF16)
    gate = _dot(xb, wg_ref[...])
    up = _dot(xb, wu_ref[...])
    hidden = (gate * jax.nn.sigmoid(gate) * up).astype(BF16)
    y = alpha * x + 0.5 * _dot(hidden, wd_ref[...])
    o_ref[...] = _layer_norm(y, g_ref[...], b_ref[...])


def _resident(shape):
    return pl.BlockSpec(shape, lambda *_: (0,) * len(shape), pipeline_mode=pl.Buffered(1))


def _ffn_ln(x2d, wg, wu, wd, g, b, *, alpha):
    t, d = x2d.shape
    tm = TM_FFN
    return pl.pallas_call(
        functools.partial(_ffn_kernel, alpha=alpha),
        grid=(t // tm,),
        in_specs=[
            pl.BlockSpec((tm, d), lambda i: (i, 0)),
            _resident(wg.shape), _resident(wu.shape), _resident(wd.shape),
            _resident(g.shape), _resident(b.shape),
        ],
        out_specs=pl.BlockSpec((tm, d), lambda i: (i, 0)),
        out_shape=jax.ShapeDtypeStruct((t, d), F32),
        compiler_params=pltpu.CompilerParams(
            dimension_semantics=("parallel",), vmem_limit_bytes=VMEM_LIMIT),
        name="ffn_ln",
    )(x2d, wg, wu, wd, g, b)


def _mixer_in_kernel(h_ref, wqk_ref, wfvt_ref, wff_ref, fb_ref, wd_ref, wdvt_ref, tri_ref,
                     eqk_ref, ones_ref, ext2_ref,
                     qa_ref, ka_ref, vft_ref, q2_ref, k2_ref, vdt_ref, carry_ref,
                     *, fox_heads, diff_heads, diff_scale, fox_scale):
    s = pl.program_id(1)
    tm = h_ref.shape[0]
    hb = h_ref[...].astype(BF16)

    @pl.when(s == 0)
    def _():
        carry_ref[...] = jnp.zeros_like(carry_ref)

    fx = _dot(hb, wff_ref[...]) + fb_ref[...]
    log_f = -(jnp.maximum(-fx, 0.0) + jnp.log1p(jnp.exp(-jnp.abs(fx))))
    lane = lax.broadcasted_iota(jnp.int32, log_f.shape, 1)
    log_f = jnp.where(lane < fox_heads, log_f, 0.0)
    t3 = jnp.concatenate(_split3(log_f), axis=1)
    cs = _dot(tri_ref[...], t3)
    c = cs[:, :LANES] + cs[:, LANES:2 * LANES] + cs[:, 2 * LANES:] + carry_ref[...]
    carry_ref[...] = c[tm - 1:tm, :]
    c3 = jnp.concatenate(_split3(c * LOG2E), axis=1)
    ext = _dot(c3, eqk_ref[...]) + ones_ref[...]

    nq = fox_heads * LANES
    pqk = _dot(hb, wqk_ref[...])
    qa = (pqk[:, :nq] * (fox_scale * LOG2E) + ext[:, :nq]).astype(BF16)
    ka = (pqk[:, nq:] + ext[:, nq:]).astype(BF16)
    for h in range(fox_heads):
        qa_ref[h] = qa[:, h * LANES:(h + 1) * LANES]
        ka_ref[h] = ka[:, h * LANES:(h + 1) * LANES]

    ones_rows = jnp.ones((DV_AUG - HEAD_DIM, TK), BF16)

    def store_values(vt, v_ref, heads):
        for cblk in range(tm // TK):
            for h in range(heads):
                v_ref[cblk, pl.ds(h * DV_AUG, HEAD_DIM), :] = (
                    vt[h * HEAD_DIM:(h + 1) * HEAD_DIM, cblk * TK:(cblk + 1) * TK])
                v_ref[cblk, pl.ds(h * DV_AUG + HEAD_DIM, DV_AUG - HEAD_DIM), :] = ones_rows

    store_values(_nt_dot(wfvt_ref[...], hb).astype(BF16), vft_ref, fox_heads)

    nd = diff_heads * LANES
    pd = _dot(hb, wd_ref[...])
    ext2 = ext2_ref[...].astype(F32)
    q2 = (pd[:, :nd] * (diff_scale * LOG2E) + ext2[:, :nd]).astype(BF16)
    k2 = (pd[:, nd:] + ext2[:, nd:]).astype(BF16)
    for h in range(diff_heads):
        q2_ref[h] = q2[:, h * LANES:(h + 1) * LANES]
        k2_ref[h] = k2[:, h * LANES:(h + 1) * LANES]

    store_values(_nt_dot(wdvt_ref[...], hb).astype(BF16), vdt_ref, diff_heads)


def _mixer_in(h, wqk, wfvt, wff, fb, wd, wdvt, tri, eqk, ones, ext2, *, fox_heads, diff_heads):
    b, s, d = h.shape
    tm = TM_IN
    nkv = s // TK
    fw = fox_heads * DV_AUG
    dw = diff_heads * DV_AUG
    const = lambda shape: pl.BlockSpec(shape, lambda bi, si: (0,) * len(shape))
    kern = functools.partial(
        _mixer_in_kernel, fox_heads=fox_heads, diff_heads=diff_heads,
        diff_scale=DIFF_QK_DIM ** -0.5, fox_scale=HEAD_DIM ** -0.5)
    return pl.pallas_call(
        kern,
        grid=(b, s // tm),
        in_specs=[
            pl.BlockSpec((None, tm, d), lambda bi, si: (bi, si, 0)),
            const(wqk.shape), const(wfvt.shape), const(wff.shape), const(fb.shape),
            const(wd.shape), const(wdvt.shape), const(tri.shape), const(eqk.shape),
            const(ones.shape),
            pl.BlockSpec((tm, ext2.shape[1]), lambda bi, si: (si, 0)),
        ],
        out_specs=[
            pl.BlockSpec((None, fox_heads, tm, LANES), lambda bi, si: (bi, 0, si, 0)),
            pl.BlockSpec((None, fox_heads, tm, LANES), lambda bi, si: (bi, 0, si, 0)),
            pl.BlockSpec((None, tm // TK, fw, TK), lambda bi, si: (bi, si, 0, 0)),
            pl.BlockSpec((None, diff_heads, tm, LANES), lambda bi, si: (bi, 0, si, 0)),
            pl.BlockSpec((None, diff_heads, tm, LANES), lambda bi, si: (bi, 0, si, 0)),
            pl.BlockSpec((None, tm // TK, dw, TK), lambda bi, si: (bi, si, 0, 0)),
        ],
        out_shape=[
            jax.ShapeDtypeStruct((b, fox_heads, s, LANES), BF16),
            jax.ShapeDtypeStruct((b, fox_heads, s, LANES), BF16),
            jax.ShapeDtypeStruct((b, nkv, fw, TK), BF16),
            jax.ShapeDtypeStruct((b, diff_heads, s, LANES), BF16),
            jax.ShapeDtypeStruct((b, diff_heads, s, LANES), BF16),
            jax.ShapeDtypeStruct((b, nkv, dw, TK), BF16),
        ],
        scratch_shapes=[pltpu.VMEM((1, LANES), F32)],
        compiler_params=pltpu.CompilerParams(
            dimension_semantics=("parallel", "arbitrary"), vmem_limit_bytes=VMEM_LIMIT),
        name="mixer_in",
    )(h, wqk, wfvt, wff, fb, wd, wdvt, tri, eqk, ones, ext2)


def _gelu_tanh(x):
    c = math.sqrt(2.0 / math.pi)
    return 0.5 * x * (1.0 + jnp.tanh(c * (x + 0.044715 * (x * x * x))))


def _sgu_kernel(h_ref, wu_ref, wg_ref, lg_ref, lb_ref, ws_ref, bs_ref, o_ref, *, heads):
    tm = h_ref.shape[0]
    hb = h_ref[...].astype(BF16)
    u = _gelu_tanh(_dot(hb, wu_ref[...]))
    g = _gelu_tanh(_dot(hb, wg_ref[...]))
    g = _layer_norm(g, lg_ref[...], lb_ref[...]).astype(BF16)
    row = lax.broadcasted_iota(jnp.int32, (CHUNK, CHUNK), 0)
    col = lax.broadcasted_iota(jnp.int32, (CHUNK, CHUNK), 1)
    causal = col <= row
    w = [jnp.where(causal, ws_ref[hh], 0.0).astype(BF16) for hh in range(heads)]
    lane = lax.broadcasted_iota(jnp.int32, (CHUNK, LANES), 1)
    low = lane < SGU_HEAD_DIM
    bias = bs_ref[...]
    for cblk in range(tm // CHUNK):
        rows = slice(cblk * CHUNK, (cblk + 1) * CHUNK)
        for p in range(heads // 2):
            cols = slice(p * LANES, (p + 1) * LANES)
            gp = g[rows, cols]
            zero = jnp.zeros_like(gp)
            mixed = (_dot(w[2 * p], jnp.where(low, gp, zero))
                     + _dot(w[2 * p + 1], jnp.where(low, zero, gp))
                     + bias[:, cols])
            o_ref[rows, cols] = (u[rows, cols] * mixed).astype(o_ref.dtype)


def _sgu(h2d, wu, wg, lg, lb, ws, bs_exp):
    t, d = h2d.shape
    width = wu.shape[1]
    heads = ws.shape[0]
    tm = TM_SGU
    const = lambda shape: pl.BlockSpec(shape, lambda i: (0,) * len(shape))
    return pl.pallas_call(
        functools.partial(_sgu_kernel, heads=heads),
        grid=(t // tm,),
        in_specs=[
            pl.BlockSpec((tm, d), lambda i: (i, 0)),
            const(wu.shape), const(wg.shape), const(lg.shape), const(lb.shape),
            const(ws.shape), const(bs_exp.shape),
        ],
        out_specs=pl.BlockSpec((tm, width), lambda i: (i, 0)),
        out_shape=jax.ShapeDtypeStruct((t, width), BF16),
        compiler_params=pltpu.CompilerParams(
            dimension_semantics=("parallel",), vmem_limit_bytes=VMEM_LIMIT),
        name="sgu",
    )(h2d, wu, wg, lg, lb, ws, bs_exp)


def _flash_sweep(qts, k_refs, v_refs, qi, scratch):
    n = len(qts)
    assert TQ == 2 * TK
    s_refs, bm_refs, p_refs, m_ref, acc_ref = scratch[0:2], scratch[2:4], scratch[4:6], scratch[6], scratch[7]
    qt_ref = scratch[8]
    for i in range(n):
        qt_ref[i] = qts[i]

    def scores(i, j, slot):
        start = pl.multiple_of(j * TK, TK)
        st = _dot(k_refs[i][pl.ds(start, TK), :], qt_ref[i])
        s_refs[slot][i] = st
        bm_refs[slot][i] = jnp.max(st, axis=0, keepdims=True)

    def step(i, j, slot, *, key_offset, issue_next):
        pv = _dot(v_refs[i][jnp.maximum(j - 1, 0)], p_refs[1 - slot][i])
        st = s_refs[slot][i]
        m = m_ref[i]
        if key_offset is None:
            m_new = jnp.maximum(m, bm_refs[slot][i])
        else:
            key = lax.broadcasted_iota(jnp.int32, st.shape, 0) + key_offset
            qry = lax.broadcasted_iota(jnp.int32, st.shape, 1)
            st = jnp.where(key <= qry, st, NEG_INF)
            m_new = jnp.maximum(m, jnp.max(st, axis=0, keepdims=True))
        p = jnp.exp2(st - m_new).astype(BF16)
        m_ref[i] = m_new
        acc_ref[i] = (acc_ref[i] + pv) * jnp.exp2(m - m_new)
        if issue_next:
            scores(i, j + 1, 1 - slot)
        return p

    for i in range(n):
        scores(i, 0, 0)
        p_refs[1][i] = jnp.zeros(p_refs[1].shape[1:], BF16)
        m_ref[i] = jnp.full(m_ref.shape[1:], NEG_INF, F32)
        acc_ref[i] = jnp.zeros(acc_ref.shape[1:], F32)

    @pl.loop(0, qi)
    def _(u):
        for i in range(n):
            p_refs[0][i] = step(i, 2 * u, 0, key_offset=None, issue_next=True)
        for i in range(n):
            p_refs[1][i] = step(i, 2 * u + 1, 1, key_offset=None, issue_next=True)

    for i in range(n):
        p_refs[0][i] = step(i, 2 * qi, 0, key_offset=0, issue_next=True)
    for i in range(n):
        p = step(i, 2 * qi + 1, 1, key_offset=TK, issue_next=False)
        acc_ref[i] = acc_ref[i] + _dot(v_refs[i][2 * qi + 1], p)


def _sweep_scratch(n):
    return ([pltpu.VMEM((n, TK, TQ), F32)] * 2 + [pltpu.VMEM((n, 1, TQ), F32)] * 2
            + [pltpu.VMEM((n, TK, TQ), BF16)] * 2
            + [pltpu.VMEM((n, 1, TQ), F32), pltpu.VMEM((n, DV_AUG, TQ), F32),
               pltpu.VMEM((n, LANES, TQ), BF16)])


def _transposed(q):
    return jnp.transpose(q.astype(F32)).astype(BF16)


def _normalised(acc_ref, i):
    return acc_ref[i, :HEAD_DIM, :] / acc_ref[i, HEAD_DIM:HEAD_DIM + 1, :]


def _fox_kernel(q_ref, k_ref, v_ref, o_ref, *scratch, hp):
    qi = pl.program_id(2)
    qts = [_transposed(q_ref[hh]) for hh in range(hp)]
    k_refs = [k_ref.at[hh] for hh in range(hp)]
    v_refs = [v_ref.at[:, pl.ds(hh * DV_AUG, DV_AUG), :] for hh in range(hp)]
    _flash_sweep(qts, k_refs, v_refs, qi, scratch)
    for hh in range(hp):
        o_ref[pl.ds(hh * HEAD_DIM, HEAD_DIM), :] = _normalised(scratch[7], hh).astype(o_ref.dtype)


def _fox_attention(qa, ka, vft, *, hp=3):
    b, heads, s, _ = qa.shape
    nkv = vft.shape[1]
    return pl.pallas_call(
        functools.partial(_fox_kernel, hp=hp),
        grid=(b, heads // hp, s // TQ),
        in_specs=[
            pl.BlockSpec((None, hp, TQ, LANES), lambda bi, hi, qi: (bi, hi, qi, 0)),
            pl.BlockSpec((None, hp, s, LANES), lambda bi, hi, qi: (bi, hi, 0, 0)),
            pl.BlockSpec((None, nkv, hp * DV_AUG, TK), lambda bi, hi, qi: (bi, 0, hi, 0)),
        ],
        out_specs=pl.BlockSpec((None, hp * HEAD_DIM, TQ), lambda bi, hi, qi: (bi, hi, qi)),
        out_shape=jax.ShapeDtypeStruct((b, heads * HEAD_DIM, s), BF16),
        scratch_shapes=_sweep_scratch(hp),
        compiler_params=pltpu.CompilerParams(
            dimension_semantics=("parallel", "parallel", "arbitrary"),
            vmem_limit_bytes=VMEM_LIMIT),
        name="fox_attention",
    )(qa, ka, vft)


def _diff_kernel(q_ref, k_ref, v_ref, lq1_ref, lk1_ref, lq2_ref, lk2_ref, g_ref, o_ref,
                 *scratch, hp, lam_init):
    qi = pl.program_id(2)
    lane = lax.broadcasted_iota(jnp.int32, (TQ, LANES), 1)
    first_map = lane < 2 * DIFF_QK_DIM
    qts, k_refs, v_refs = [], [], []
    for hh in range(hp):
        q2 = q_ref[hh]
        zero = jnp.zeros_like(q2)
        qts += [_transposed(jnp.where(first_map, q2, zero)), _transposed(jnp.where(first_map, zero, q2))]
        k_refs += [k_ref.at[hh]] * 2
        v_refs += [v_ref.at[:, pl.ds(hh * DV_AUG, DV_AUG), :]] * 2
    _flash_sweep(qts, k_refs, v_refs, qi, scratch)

    lam = (jnp.exp(jnp.sum(lq1_ref[...] * lk1_ref[...], axis=-1, keepdims=True))
           - jnp.exp(jnp.sum(lq2_ref[...] * lk2_ref[...], axis=-1, keepdims=True))
           + lam_init)
    for hh in range(hp):
        o = _normalised(scratch[7], 2 * hh) - lam * _normalised(scratch[7], 2 * hh + 1)
        ms = jnp.mean(o * o, axis=0, keepdims=True)
        o = o * lax.rsqrt(ms + LN_EPS)
        o = o * g_ref[hh] * (1.0 - lam_init)
        o_ref[pl.ds(hh * DIFF_V_DIM, DIFF_V_DIM), :] = o.astype(o_ref.dtype)


def _diff_attention(q2, k2, vdt, lq1, lk1, lq2, lk2, gcol, *, lam_init, hp=2):
    b, heads, s, _ = q2.shape
    nkv = vdt.shape[1]
    vec = lambda a: pl.BlockSpec(a.shape, lambda bi, hi, qi: (0, 0))
    return pl.pallas_call(
        functools.partial(_diff_kernel, hp=hp, lam_init=lam_init),
        grid=(b, heads // hp, s // TQ),
        in_specs=[
            pl.BlockSpec((None, hp, TQ, LANES), lambda bi, hi, qi: (bi, hi, qi, 0)),
            pl.BlockSpec((None, hp, s, LANES), lambda bi, hi, qi: (bi, hi, 0, 0)),
            pl.BlockSpec((None, nkv, hp * DV_AUG, TK), lambda bi, hi, qi: (bi, 0, hi, 0)),
            vec(lq1), vec(lk1), vec(lq2), vec(lk2),
            pl.BlockSpec((hp, DIFF_V_DIM, 1), lambda bi, hi, qi: (hi, 0, 0)),
        ],
        out_specs=pl.BlockSpec((None, hp * DIFF_V_DIM, TQ), lambda bi, hi, qi: (bi, hi, qi)),
        out_shape=jax.ShapeDtypeStruct((b, heads * DIFF_V_DIM, s), BF16),
        scratch_shapes=_sweep_scratch(2 * hp),
        compiler_params=pltpu.CompilerParams(
            dimension_semantics=("parallel", "parallel", "arbitrary"),
            vmem_limit_bytes=VMEM_LIMIT),
        name="diff_attention",
    )(q2, k2, vdt, lq1, lk1, lq2, lk2, gcol)


def _mixer_out_kernel(yf_ref, yd_ref, ys_ref, h_ref, w_ref, g_ref, b_ref, o_ref, *, alpha):
    yf = jnp.transpose(yf_ref[...].astype(F32)).astype(BF16)
    yd = jnp.transpose(yd_ref[...].astype(F32)).astype(BF16)
    y = jnp.concatenate([yf, yd, ys_ref[...]], axis=1)
    m = _dot(y, w_ref[...])
    o_ref[...] = _layer_norm(alpha * h_ref[...] + m, g_ref[...], b_ref[...])


def _mixer_out(yft, ydt, ys, h, w, g, b, *, alpha):
    bsz, s, d = h.shape
    fw, dw, sw = yft.shape[1], ydt.shape[1], ys.shape[2]
    tm = TM_OUT
    const = lambda shape: pl.BlockSpec(shape, lambda bi, si: (0,) * len(shape))
    return pl.pallas_call(
        functools.partial(_mixer_out_kernel, alpha=alpha),
        grid=(bsz, s // tm),
        in_specs=[
            pl.BlockSpec((None, fw, tm), lambda bi, si: (bi, 0, si)),
            pl.BlockSpec((None, dw, tm), lambda bi, si: (bi, 0, si)),
            pl.BlockSpec((None, tm, sw), lambda bi, si: (bi, si, 0)),
            pl.BlockSpec((None, tm, d), lambda bi, si: (bi, si, 0)),
            const(w.shape), const(g.shape), const(b.shape),
        ],
        out_specs=pl.BlockSpec((None, tm, d), lambda bi, si: (bi, si, 0)),
        out_shape=jax.ShapeDtypeStruct((bsz, s, d), F32),
        compiler_params=pltpu.CompilerParams(
            dimension_semantics=("parallel", "parallel"), vmem_limit_bytes=VMEM_LIMIT),
        name="mixer_out",
    )(yft, ydt, ys, h, w, g, b)


def _fox_bias_tables(fox_heads):
    nq = fox_heads * LANES
    eqk = np.zeros((3 * LANES, 2 * nq), np.float32)
    ones = np.zeros((1, 2 * nq), np.float32)
    for h in range(fox_heads):
        for t in range(3):
            ones[0, h * LANES + FOX_EXT + t] = 1.0
            eqk[t * LANES + h, h * LANES + FOX_EXT + 3 + t] = 1.0
            eqk[t * LANES + h, nq + h * LANES + FOX_EXT + t] = -1.0
            ones[0, nq + h * LANES + FOX_EXT + 3 + t] = 1.0
    return jnp.asarray(eqk, BF16), jnp.asarray(ones, F32)


def _alibi_tables(seq, diff_heads):
    nd = diff_heads * LANES
    ext = np.zeros((seq, 2 * nd), np.float32)
    pos = np.arange(seq, dtype=np.float32)
    slopes = np.float32(2.0) ** (np.float32(-8.0) * np.arange(1, diff_heads + 1, dtype=np.float32)
                                 / np.float32(diff_heads))
    for h in range(diff_heads):
        q_terms = _split3_np(np.float32(LOG2E) * (-slopes[h] * pos))
        k_terms = _split3_np(np.float32(LOG2E) * (slopes[h] * pos))
        for mp in range(2):
            base = h * LANES + mp * 2 * DIFF_QK_DIM + DIFF_EXT
            for t in range(3):
                ext[:, base + t] = 1.0
                ext[:, base + 3 + t] = q_terms[t]
                ext[:, nd + base + t] = k_terms[t]
                ext[:, nd + base + 3 + t] = 1.0
    return jnp.asarray(ext, BF16)


def _pad_heads(w, heads, dim):
    d = w.shape[0]
    return jnp.pad(w.reshape(d, heads, dim), ((0, 0), (0, 0), (0, LANES - dim))).reshape(d, heads * LANES)


def _pad_maps(w, heads):
    d = w.shape[0]
    w = w.reshape(d, heads, 2, DIFF_QK_DIM)
    w = jnp.pad(w, ((0, 0), (0, 0), (0, 0), (0, 2 * DIFF_QK_DIM - DIFF_QK_DIM)))
    return w.reshape(d, heads * LANES)


def kernel(x, ffn_a_w_gate, ffn_a_w_up, ffn_a_w_down, norm_a_g, norm_a_b, w_in, fox_f_bias, diff_lambda_q1, diff_lambda_k1, diff_lambda_q2, diff_lambda_k2, diff_norm_g, sgu_norm_g, sgu_norm_b, sgu_w_s, sgu_b_s, w_out, norm_m_g, norm_m_b, ffn_b_w_gate, ffn_b_w_up, ffn_b_w_down, norm_b_g, norm_b_b):
    bsz, seq, d = x.shape
    depth = w_in.shape[0]
    fox_heads = fox_f_bias.shape[1]
    fox_w = fox_heads * HEAD_DIM
    diff_w = diff_norm_g.shape[1]
    diff_heads = diff_w // DIFF_V_DIM
    sgu_w = sgu_norm_g.shape[1]
    sgu_heads = sgu_w_s.shape[1]
    alpha = (2 * depth) ** 0.25

    eqk, ones = _fox_bias_tables(fox_heads)
    ext2 = _alibi_tables(seq, diff_heads)
    tri = jnp.asarray(np.tril(np.ones((TM_IN, TM_IN), np.float32)), BF16)

    row = lambda v: v.reshape(1, -1)
    h = x
    for l in range(depth):
        lam_init = 0.8 - 0.6 * math.exp(-0.3 * l)
        h = _ffn_ln(h.reshape(bsz * seq, d), ffn_a_w_gate[l].astype(BF16), ffn_a_w_up[l].astype(BF16),
                    ffn_a_w_down[l].astype(BF16), row(norm_a_g[l]), row(norm_a_b[l]),
                    alpha=alpha).reshape(bsz, seq, d)

        wl = w_in[l]
        i1 = 3 * fox_w
        i2 = i1 + fox_heads
        i3 = i2 + 3 * diff_w
        wqk = jnp.concatenate([_pad_heads(wl[:, :fox_w], fox_heads, HEAD_DIM),
                               _pad_heads(wl[:, fox_w:2 * fox_w], fox_heads, HEAD_DIM)], axis=1).astype(BF16)
        wfvt = wl[:, 2 * fox_w:i1].T.astype(BF16)
        wff = jnp.pad(wl[:, i1:i2], ((0, 0), (0, LANES - fox_heads))).astype(BF16)
        fb = jnp.pad(fox_f_bias[l], (0, LANES - fox_heads)).reshape(1, LANES)
        wdqk = jnp.concatenate([_pad_maps(wl[:, i2:i2 + diff_w], diff_heads),
                                _pad_maps(wl[:, i2 + diff_w:i2 + 2 * diff_w], diff_heads)], axis=1).astype(BF16)
        wdvt = wl[:, i2 + 2 * diff_w:i3].T.astype(BF16)
        qa, ka, vft, q2, k2, vdt = _mixer_in(h, wqk, wfvt, wff, fb, wdqk, wdvt, tri, eqk, ones, ext2,
                                              fox_heads=fox_heads, diff_heads=diff_heads)

        bs_exp = jnp.repeat(jnp.transpose(sgu_b_s[l]), SGU_HEAD_DIM, axis=1)
        ys = _sgu(h.reshape(bsz * seq, d), wl[:, i3:i3 + sgu_w].astype(BF16), wl[:, i3 + sgu_w:].astype(BF16),
                  row(sgu_norm_g[l]), row(sgu_norm_b[l]), sgu_w_s[l], bs_exp).reshape(bsz, seq, sgu_w)

        yft = _fox_attention(qa, ka, vft)
        ydt = _diff_attention(q2, k2, vdt, row(diff_lambda_q1[l]), row(diff_lambda_k1[l]),
                              row(diff_lambda_q2[l]), row(diff_lambda_k2[l]),
                              diff_norm_g[l].reshape(diff_heads, DIFF_V_DIM, 1), lam_init=lam_init)

        h = _mixer_out(yft, ydt, ys, h, w_out[l].astype(BF16), row(norm_m_g[l]), row(norm_m_b[l]), alpha=alpha)

        h = _ffn_ln(h.reshape(bsz * seq, d), ffn_b_w_gate[l].astype(BF16), ffn_b_w_up[l].astype(BF16),
                    ffn_b_w_down[l].astype(BF16), row(norm_b_g[l]), row(norm_b_b[l]),
                    alpha=alpha).reshape(bsz, seq, d)
    return h
```

```python
import functools
import math

import numpy as np
import jax
import jax.numpy as jnp
from jax import lax
from jax.experimental import pallas as pl
from jax.experimental.pallas import tpu as pltpu

F32 = jnp.float32
BF16 = jnp.bfloat16

HEAD_DIM = 64
DIFF_QK_DIM = 32
DIFF_V_DIM = 64
SGU_HEAD_DIM = 64
CHUNK = 128
LN_EPS = 1e-5
NEG_INF = -1e30
LOG2E = math.log2(math.e)
DV_AUG = HEAD_DIM + 16

LANES = 128
VMEM_LIMIT = 56 * 1024 * 1024

FOX_EXT = HEAD_DIM
DIFF_EXT = DIFF_QK_DIM

TQ = 512
TK = 256
TM_FFN = 512
TM_IN = 512
TM_SGU = 512
TM_OUT = 512


def _nt_dot(a, b):
    return lax.dot_general(a, b, (((1,), (1,)), ((), ())), preferred_element_type=F32)


def _dot(a, b):
    return jnp.dot(a, b, preferred_element_type=F32)


def _layer_norm(y, g, b):
    mu = jnp.mean(y, axis=-1, keepdims=True)
    d = y - mu
    var = jnp.mean(d * d, axis=-1, keepdims=True)
    return d * lax.rsqrt(var + LN_EPS) * g + b


def _split3(x):
    hi = x.astype(BF16)
    r1 = x - hi.astype(F32)
    mid = r1.astype(BF16)
    r2 = r1 - mid.astype(F32)
    lo = r2.astype(BF16)
    return hi, mid, lo


def _split3_np(x):
    x = np.asarray(x, np.float32)
    hi = x.astype(BF16).astype(np.float32)
    r1 = x - hi
    mid = r1.astype(BF16).astype(np.float32)
    lo = (r1 - mid).astype(BF16).astype(np.float32)
    return hi, mid, lo


def _ffn_kernel(x_ref, wg_ref, wu_ref, wd_ref, g_ref, b_ref, o_ref, *, alpha):
    x = x_ref[...]
    xb = x.astype(BF16)
    gate = _dot(xb, wg_ref[...])
    up = _dot(xb, wu_ref[...])
    hidden = (gate * jax.nn.sigmoid(gate) * up).astype(BF16)
    y = alpha * x + 0.5 * _dot(hidden, wd_ref[...])
    o_ref[...] = _layer_norm(y, g_ref[...], b_ref[...])


def _resident(shape):
    return pl.BlockSpec(shape, lambda *_: (0,) * len(shape), pipeline_mode=pl.Buffered(1))


def _ffn_ln(x2d, wg, wu, wd, g, b, *, alpha):
    t, d = x2d.shape
    tm = TM_FFN
    return pl.pallas_call(
        functools.partial(_ffn_kernel, alpha=alpha),
        grid=(t // tm,),
        in_specs=[
            pl.BlockSpec((tm, d), lambda i: (i, 0)),
            _resident(wg.shape), _resident(wu.shape), _resident(wd.shape),
            _resident(g.shape), _resident(b.shape),
        ],
        out_specs=pl.BlockSpec((tm, d), lambda i: (i, 0)),
        out_shape=jax.ShapeDtypeStruct((t, d), F32),
        compiler_params=pltpu.CompilerParams(
            dimension_semantics=("parallel",), vmem_limit_bytes=VMEM_LIMIT),
        name="ffn_ln",
    )(x2d, wg, wu, wd, g, b)


def _mixer_in_kernel(h_ref, wqk_ref, wfvt_ref, wff_ref, fb_ref, wd_ref, wdvt_ref, tri_ref,
                     eqk_ref, ones_ref, ext2_ref,
                     qa_ref, ka_ref, vft_ref, q2_ref, k2_ref, vdt_ref, carry_ref,
                     *, fox_heads, diff_heads, diff_scale, fox_scale):
    s = pl.program_id(1)
    tm = h_ref.shape[0]
    hb = h_ref[...].astype(BF16)

    @pl.when(s == 0)
    def _():
        carry_ref[...] = jnp.zeros_like(carry_ref)

    fx = _dot(hb, wff_ref[...]) + fb_ref[...]
    log_f = -(jnp.maximum(-fx, 0.0) + jnp.log1p(jnp.exp(-jnp.abs(fx))))
    lane = lax.broadcasted_iota(jnp.int32, log_f.shape, 1)
    log_f = jnp.where(lane < fox_heads, log_f, 0.0)
    t3 = jnp.concatenate(_split3(log_f), axis=1)
    cs = _dot(tri_ref[...], t3)
    c = cs[:, :LANES] + cs[:, LANES:2 * LANES] + cs[:, 2 * LANES:] + carry_ref[...]
    carry_ref[...] = c[tm - 1:tm, :]
    c3 = jnp.concatenate(_split3(c * LOG2E), axis=1)
    ext = _dot(c3, eqk_ref[...]) + ones_ref[...]

    nq = fox_heads * LANES
    pqk = _dot(hb, wqk_ref[...])
    qa = (pqk[:, :nq] * (fox_scale * LOG2E) + ext[:, :nq]).astype(BF16)
    ka = (pqk[:, nq:] + ext[:, nq:]).astype(BF16)
    for h in range(fox_heads):
        qa_ref[h] = qa[:, h * LANES:(h + 1) * LANES]
        ka_ref[h] = ka[:, h * LANES:(h + 1) * LANES]

    ones_rows = jnp.ones((DV_AUG - HEAD_DIM, TK), BF16)

    def store_values(vt, v_ref, heads):
        for cblk in range(tm // TK):
            for h in range(heads):
                v_ref[cblk, pl.ds(h * DV_AUG, HEAD_DIM), :] = (
                    vt[h * HEAD_DIM:(h + 1) * HEAD_DIM, cblk * TK:(cblk + 1) * TK])
                v_ref[cblk, pl.ds(h * DV_AUG + HEAD_DIM, DV_AUG - HEAD_DIM), :] = ones_rows

    store_values(_nt_dot(wfvt_ref[...], hb).astype(BF16), vft_ref, fox_heads)

    nd = diff_heads * LANES
    pd = _dot(hb, wd_ref[...])
    ext2 = ext2_ref[...].astype(F32)
    q2 = (pd[:, :nd] * (diff_scale * LOG2E) + ext2[:, :nd]).astype(BF16)
    k2 = (pd[:, nd:] + ext2[:, nd:]).astype(BF16)
    for h in range(diff_heads):
        q2_ref[h] = q2[:, h * LANES:(h + 1) * LANES]
        k2_ref[h] = k2[:, h * LANES:(h + 1) * LANES]

    store_values(_nt_dot(wdvt_ref[...], hb).astype(BF16), vdt_ref, diff_heads)


def _mixer_in(h, wqk, wfvt, wff, fb, wd, wdvt, tri, eqk, ones, ext2, *, fox_heads, diff_heads):
    b, s, d = h.shape
    tm = TM_IN
    nkv = s // TK
    fw = fox_heads * DV_AUG
    dw = diff_heads * DV_AUG
    const = lambda shape: pl.BlockSpec(shape, lambda bi, si: (0,) * len(shape))
    kern = functools.partial(
        _mixer_in_kernel, fox_heads=fox_heads, diff_heads=diff_heads,
        diff_scale=DIFF_QK_DIM ** -0.5, fox_scale=HEAD_DIM ** -0.5)
    return pl.pallas_call(
        kern,
        grid=(b, s // tm),
        in_specs=[
            pl.BlockSpec((None, tm, d), lambda bi, si: (bi, si, 0)),
            const(wqk.shape), const(wfvt.shape), const(wff.shape), const(fb.shape),
            const(wd.shape), const(wdvt.shape), const(tri.shape), const(eqk.shape),
            const(ones.shape),
            pl.BlockSpec((tm, ext2.shape[1]), lambda bi, si: (si, 0)),
        ],
        out_specs=[
            pl.BlockSpec((None, fox_heads, tm, LANES), lambda bi, si: (bi, 0, si, 0)),
            pl.BlockSpec((None, fox_heads, tm, LANES), lambda bi, si: (bi, 0, si, 0)),
            pl.BlockSpec((None, tm // TK, fw, TK), lambda bi, si: (bi, si, 0, 0)),
            pl.BlockSpec((None, diff_heads, tm, LANES), lambda bi, si: (bi, 0, si, 0)),
            pl.BlockSpec((None, diff_heads, tm, LANES), lambda bi, si: (bi, 0, si, 0)),
            pl.BlockSpec((None, tm // TK, dw, TK), lambda bi, si: (bi, si, 0, 0)),
        ],
        out_shape=[
            jax.ShapeDtypeStruct((b, fox_heads, s, LANES), BF16),
            jax.ShapeDtypeStruct((b, fox_heads, s, LANES), BF16),
            jax.ShapeDtypeStruct((b, nkv, fw, TK), BF16),
            jax.ShapeDtypeStruct((b, diff_heads, s, LANES), BF16),
            jax.ShapeDtypeStruct((b, diff_heads, s, LANES), BF16),
            jax.ShapeDtypeStruct((b, nkv, dw, TK), BF16),
        ],
        scratch_shapes=[pltpu.VMEM((1, LANES), F32)],
        compiler_params=pltpu.CompilerParams(
            dimension_semantics=("parallel", "arbitrary"), vmem_limit_bytes=VMEM_LIMIT),
        name="mixer_in",
    )(h, wqk, wfvt, wff, fb, wd, wdvt, tri, eqk, ones, ext2)


def _gelu_tanh(x):
    c = math.sqrt(2.0 / math.pi)
    return 0.5 * x * (1.0 + jnp.tanh(c * (x + 0.044715 * (x * x * x))))


def _sgu_kernel(h_ref, wu_ref, wg_ref, lg_ref, lb_ref, ws_ref, bs_ref, o_ref, *, heads):
    tm = h_ref.shape[0]
    hb = h_ref[...].astype(BF16)
    u = _gelu_tanh(_dot(hb, wu_ref[...]))
    g = _gelu_tanh(_dot(hb, wg_ref[...]))
    g = _layer_norm(g, lg_ref[...], lb_ref[...]).astype(BF16)
    row = lax.broadcasted_iota(jnp.int32, (CHUNK, CHUNK), 0)
    col = lax.broadcasted_iota(jnp.int32, (CHUNK, CHUNK), 1)
    causal = col <= row
    w = [jnp.where(causal, ws_ref[hh], 0.0).astype(BF16) for hh in range(heads)]
    lane = lax.broadcasted_iota(jnp.int32, (CHUNK, LANES), 1)
    low = lane < SGU_HEAD_DIM
    bias = bs_ref[...]
    for cblk in range(tm // CHUNK):
        rows = slice(cblk * CHUNK, (cblk + 1) * CHUNK)
        for p in range(heads // 2):
            cols = slice(p * LANES, (p + 1) * LANES)
            gp = g[rows, cols]
            zero = jnp.zeros_like(gp)
            mixed = (_dot(w[2 * p], jnp.where(low, gp, zero))
                     + _dot(w[2 * p + 1], jnp.where(low, zero, gp))
                     + bias[:, cols])
            o_ref[rows, cols] = (u[rows, cols] * mixed).astype(o_ref.dtype)


def _sgu(h2d, wu, wg, lg, lb, ws, bs_exp):
    t, d = h2d.shape
    width = wu.shape[1]
    heads = ws.shape[0]
    tm = TM_SGU
    const = lambda shape: pl.BlockSpec(shape, lambda i: (0,) * len(shape))
    return pl.pallas_call(
        functools.partial(_sgu_kernel, heads=heads),
        grid=(t // tm,),
        in_specs=[
            pl.BlockSpec((tm, d), lambda i: (i, 0)),
            const(wu.shape), const(wg.shape), const(lg.shape), const(lb.shape),
            const(ws.shape), const(bs_exp.shape),
        ],
        out_specs=pl.BlockSpec((tm, width), lambda i: (i, 0)),
        out_shape=jax.ShapeDtypeStruct((t, width), BF16),
        compiler_params=pltpu.CompilerParams(
            dimension_semantics=("parallel",), vmem_limit_bytes=VMEM_LIMIT),
        name="sgu",
    )(h2d, wu, wg, lg, lb, ws, bs_exp)


def _flash_sweep(qts, k_refs, v_refs, qi, scratch):
    n = len(qts)
    assert TQ == 2 * TK
    s_refs, bm_refs, m_ref, acc_ref, qt_ref = scratch[0:2], scratch[2:4], scratch[4], scratch[5], scratch[6]
    for i in range(n):
        qt_ref[i] = qts[i]

    def scores(i, j, slot):
        start = pl.multiple_of(j * TK, TK)
        st = _dot(k_refs[i][pl.ds(start, TK), :], qt_ref[i])
        s_refs[slot][i] = st
        bm_refs[slot][i] = jnp.max(st, axis=0, keepdims=True)

    def step(i, j, slot):
        scores(i, j + 1, 1 - slot)
        m = m_ref[i]
        m_new = jnp.maximum(m, bm_refs[slot][i])
        p = jnp.exp2(s_refs[slot][i] - m_new).astype(BF16)
        m_ref[i] = m_new
        acc_ref[i] = acc_ref[i] * jnp.exp2(m - m_new) + _dot(v_refs[i][j], p)

    for i in range(n):
        scores(i, 0, 0)
        m_ref[i] = jnp.full(m_ref.shape[1:], NEG_INF, F32)
        acc_ref[i] = jnp.zeros(acc_ref.shape[1:], F32)

    @pl.loop(0, qi)
    def _(u):
        for i in range(n):
            step(i, 2 * u, 0)
        for i in range(n):
            step(i, 2 * u + 1, 1)

    causal = (lax.broadcasted_iota(jnp.int32, (TK, TK), 0) <= lax.broadcasted_iota(jnp.int32, (TK, TK), 1))
    last = 2 * qi + 1
    for i in range(n):
        start = pl.multiple_of(last * TK, TK)
        s_refs[1][i, :, TK:] = _dot(k_refs[i][pl.ds(start, TK), :], qt_ref[i, :, TK:])
        st = s_refs[0][i]
        st = jnp.concatenate([jnp.where(causal, st[:, :TK], NEG_INF), st[:, TK:]], axis=1)
        m = m_ref[i]
        m_new = jnp.maximum(m, jnp.max(st, axis=0, keepdims=True))
        p = jnp.exp2(st - m_new).astype(BF16)
        m_ref[i] = m_new
        acc_ref[i] = acc_ref[i] * jnp.exp2(m - m_new) + _dot(v_refs[i][last - 1], p)
    for i in range(n):
        st = jnp.where(causal, s_refs[1][i, :, TK:], NEG_INF)
        m = m_ref[i, :, TK:]
        m_new = jnp.maximum(m, jnp.max(st, axis=0, keepdims=True))
        p = jnp.exp2(st - m_new).astype(BF16)
        acc_ref[i, :, TK:] = acc_ref[i, :, TK:] * jnp.exp2(m - m_new) + _dot(v_refs[i][last], p)


ACC_SCRATCH = 5


def _sweep_scratch(n):
    return ([pltpu.VMEM((n, TK, TQ), F32)] * 2 + [pltpu.VMEM((n, 1, TQ), F32)] * 2
            + [pltpu.VMEM((n, 1, TQ), F32), pltpu.VMEM((n, DV_AUG, TQ), F32),
               pltpu.VMEM((n, LANES, TQ), BF16)])


def _transposed(q):
    return jnp.transpose(q.astype(F32)).astype(BF16)


def _normalised(acc_ref, i):
    return acc_ref[i, :HEAD_DIM, :] / acc_ref[i, HEAD_DIM:HEAD_DIM + 1, :]


def _fox_kernel(q_ref, k_ref, v_ref, o_ref, *scratch, hp):
    qi = pl.program_id(2)
    qts = [_transposed(q_ref[hh]) for hh in range(hp)]
    k_refs = [k_ref.at[hh] for hh in range(hp)]
    v_refs = [v_ref.at[:, pl.ds(hh * DV_AUG, DV_AUG), :] for hh in range(hp)]
    _flash_sweep(qts, k_refs, v_refs, qi, scratch)
    for hh in range(hp):
        o_ref[pl.ds(hh * HEAD_DIM, HEAD_DIM), :] = _normalised(scratch[ACC_SCRATCH], hh).astype(o_ref.dtype)


def _fox_attention(qa, ka, vft, *, hp=6):
    b, heads, s, _ = qa.shape
    nkv = vft.shape[1]
    return pl.pallas_call(
        functools.partial(_fox_kernel, hp=hp),
        grid=(b, heads // hp, s // TQ),
        in_specs=[
            pl.BlockSpec((None, hp, TQ, LANES), lambda bi, hi, qi: (bi, hi, qi, 0)),
            pl.BlockSpec((None, hp, s, LANES), lambda bi, hi, qi: (bi, hi, 0, 0)),
            pl.BlockSpec((None, nkv, hp * DV_AUG, TK), lambda bi, hi, qi: (bi, 0, hi, 0)),
        ],
        out_specs=pl.BlockSpec((None, hp * HEAD_DIM, TQ), lambda bi, hi, qi: (bi, hi, qi)),
        out_shape=jax.ShapeDtypeStruct((b, heads * HEAD_DIM, s), BF16),
        scratch_shapes=_sweep_scratch(hp),
        compiler_params=pltpu.CompilerParams(
            dimension_semantics=("parallel", "parallel", "arbitrary"),
            vmem_limit_bytes=VMEM_LIMIT),
        name="fox_attention",
    )(qa, ka, vft)


def _diff_kernel(q_ref, k_ref, v_ref, lq1_ref, lk1_ref, lq2_ref, lk2_ref, g_ref, o_ref,
                 *scratch, hp, lam_init):
    qi = pl.program_id(2)
    lane = lax.broadcasted_iota(jnp.int32, (TQ, LANES), 1)
    first_map = lane < 2 * DIFF_QK_DIM
    qts, k_refs, v_refs = [], [], []
    for hh in range(hp):
        q2 = q_ref[hh]
        zero = jnp.zeros_like(q2)
        qts += [_transposed(jnp.where(first_map, q2, zero)), _transposed(jnp.where(first_map, zero, q2))]
        k_refs += [k_ref.at[hh]] * 2
        v_refs += [v_ref.at[:, pl.ds(hh * DV_AUG, DV_AUG), :]] * 2
    _flash_sweep(qts, k_refs, v_refs, qi, scratch)

    lam = (jnp.exp(jnp.sum(lq1_ref[...] * lk1_ref[...], axis=-1, keepdims=True))
           - jnp.exp(jnp.sum(lq2_ref[...] * lk2_ref[...], axis=-1, keepdims=True))
           + lam_init)
    for hh in range(hp):
        o = _normalised(scratch[ACC_SCRATCH], 2 * hh) - lam * _normalised(scratch[ACC_SCRATCH], 2 * hh + 1)
        ms = jnp.mean(o * o, axis=0, keepdims=True)
        o = o * lax.rsqrt(ms + LN_EPS)
        o = o * g_ref[hh] * (1.0 - lam_init)
        o_ref[pl.ds(hh * DIFF_V_DIM, DIFF_V_DIM), :] = o.astype(o_ref.dtype)


def _diff_attention(q2, k2, vdt, lq1, lk1, lq2, lk2, gcol, *, lam_init, hp=4):
    b, heads, s, _ = q2.shape
    nkv = vdt.shape[1]
    vec = lambda a: pl.BlockSpec(a.shape, lambda bi, hi, qi: (0, 0))
    return pl.pallas_call(
        functools.partial(_diff_kernel, hp=hp, lam_init=lam_init),
        grid=(b, heads // hp, s // TQ),
        in_specs=[
            pl.BlockSpec((None, hp, TQ, LANES), lambda bi, hi, qi: (bi, hi, qi, 0)),
            pl.BlockSpec((None, hp, s, LANES), lambda bi, hi, qi: (bi, hi, 0, 0)),
            pl.BlockSpec((None, nkv, hp * DV_AUG, TK), lambda bi, hi, qi: (bi, 0, hi, 0)),
            vec(lq1), vec(lk1), vec(lq2), vec(lk2),
            pl.BlockSpec((hp, DIFF_V_DIM, 1), lambda bi, hi, qi: (hi, 0, 0)),
        ],
        out_specs=pl.BlockSpec((None, hp * DIFF_V_DIM, TQ), lambda bi, hi, qi: (bi, hi, qi)),
        out_shape=jax.ShapeDtypeStruct((b, heads * DIFF_V_DIM, s), BF16),
        scratch_shapes=_sweep_scratch(2 * hp),
        compiler_params=pltpu.CompilerParams(
            dimension_semantics=("parallel", "parallel", "arbitrary"),
            vmem_limit_bytes=VMEM_LIMIT),
        name="diff_attention",
    )(q2, k2, vdt, lq1, lk1, lq2, lk2, gcol)


def _mixer_out_kernel(yf_ref, yd_ref, ys_ref, h_ref, w_ref, g_ref, b_ref, o_ref, *, alpha):
    yf = jnp.transpose(yf_ref[...].astype(F32)).astype(BF16)
    yd = jnp.transpose(yd_ref[...].astype(F32)).astype(BF16)
    y = jnp.concatenate([yf, yd, ys_ref[...]], axis=1)
    m = _dot(y, w_ref[...])
    o_ref[...] = _layer_norm(alpha * h_ref[...] + m, g_ref[...], b_ref[...])


def _mixer_out(yft, ydt, ys, h, w, g, b, *, alpha):
    bsz, s, d = h.shape
    fw, dw, sw = yft.shape[1], ydt.shape[1], ys.shape[2]
    tm = TM_OUT
    const = lambda shape: pl.BlockSpec(shape, lambda bi, si: (0,) * len(shape))
    return pl.pallas_call(
        functools.partial(_mixer_out_kernel, alpha=alpha),
        grid=(bsz, s // tm),
        in_specs=[
            pl.BlockSpec((None, fw, tm), lambda bi, si: (bi, 0, si)),
            pl.BlockSpec((None, dw, tm), lambda bi, si: (bi, 0, si)),
            pl.BlockSpec((None, tm, sw), lambda bi, si: (bi, si, 0)),
            pl.BlockSpec((None, tm, d), lambda bi, si: (bi, si, 0)),
            const(w.shape), const(g.shape), const(b.shape),
        ],
        out_specs=pl.BlockSpec((None, tm, d), lambda bi, si: (bi, si, 0)),
        out_shape=jax.ShapeDtypeStruct((bsz, s, d), F32),
        compiler_params=pltpu.CompilerParams(
            dimension_semantics=("parallel", "parallel"), vmem_limit_bytes=VMEM_LIMIT),
        name="mixer_out",
    )(yft, ydt, ys, h, w, g, b)


def _fox_bias_tables(fox_heads):
    nq = fox_heads * LANES
    eqk = np.zeros((3 * LANES, 2 * nq), np.float32)
    ones = np.zeros((1, 2 * nq), np.float32)
    for h in range(fox_heads):
        for t in range(3):
            ones[0, h * LANES + FOX_EXT + t] = 1.0
            eqk[t * LANES + h, h * LANES + FOX_EXT + 3 + t] = 1.0
            eqk[t * LANES + h, nq + h * LANES + FOX_EXT + t] = -1.0
            ones[0, nq + h * LANES + FOX_EXT + 3 + t] = 1.0
    return jnp.asarray(eqk, BF16), jnp.asarray(ones, F32)


def _alibi_tables(seq, diff_heads):
    nd = diff_heads * LANES
    ext = np.zeros((seq, 2 * nd), np.float32)
    pos = np.arange(seq, dtype=np.float32)
    slopes = np.float32(2.0) ** (np.float32(-8.0) * np.arange(1, diff_heads + 1, dtype=np.float32)
                                 / np.float32(diff_heads))
    for h in range(diff_heads):
        q_terms = _split3_np(np.float32(LOG2E) * (-slopes[h] * pos))
        k_terms = _split3_np(np.float32(LOG2E) * (slopes[h] * pos))
        for mp in range(2):
            base = h * LANES + mp * 2 * DIFF_QK_DIM + DIFF_EXT
            for t in range(3):
                ext[:, base + t] = 1.0
                ext[:, base + 3 + t] = q_terms[t]
                ext[:, nd + base + t] = k_terms[t]
                ext[:, nd + base + 3 + t] = 1.0
    return jnp.asarray(ext, BF16)


def _pad_heads(w, heads, dim):
    d = w.shape[0]
    return jnp.pad(w.reshape(d, heads, dim), ((0, 0), (0, 0), (0, LANES - dim))).reshape(d, heads * LANES)


def _pad_maps(w, heads):
    d = w.shape[0]
    w = w.reshape(d, heads, 2, DIFF_QK_DIM)
    w = jnp.pad(w, ((0, 0), (0, 0), (0, 0), (0, 2 * DIFF_QK_DIM - DIFF_QK_DIM)))
    return w.reshape(d, heads * LANES)


def kernel(x, ffn_a_w_gate, ffn_a_w_up, ffn_a_w_down, norm_a_g, norm_a_b, w_in, fox_f_bias, diff_lambda_q1, diff_lambda_k1, diff_lambda_q2, diff_lambda_k2, diff_norm_g, sgu_norm_g, sgu_norm_b, sgu_w_s, sgu_b_s, w_out, norm_m_g, norm_m_b, ffn_b_w_gate, ffn_b_w_up, ffn_b_w_down, norm_b_g, norm_b_b):
    bsz, seq, d = x.shape
    depth = w_in.shape[0]
    fox_heads = fox_f_bias.shape[1]
    fox_w = fox_heads * HEAD_DIM
    diff_w = diff_norm_g.shape[1]
    diff_heads = diff_w // DIFF_V_DIM
    sgu_w = sgu_norm_g.shape[1]
    sgu_heads = sgu_w_s.shape[1]
    alpha = (2 * depth) ** 0.25

    eqk, ones = _fox_bias_tables(fox_heads)
    ext2 = _alibi_tables(seq, diff_heads)
    tri = jnp.asarray(np.tril(np.ones((TM_IN, TM_IN), np.float32)), BF16)

    row = lambda v: v.reshape(1, -1)
    h = x
    for l in range(depth):
        lam_init = 0.8 - 0.6 * math.exp(-0.3 * l)
        h = _ffn_ln(h.reshape(bsz * seq, d), ffn_a_w_gate[l].astype(BF16), ffn_a_w_up[l].astype(BF16),
                    ffn_a_w_down[l].astype(BF16), row(norm_a_g[l]), row(norm_a_b[l]),
                    alpha=alpha).reshape(bsz, seq, d)

        wl = w_in[l]
        i1 = 3 * fox_w
        i2 = i1 + fox_heads
        i3 = i2 + 3 * diff_w
        wqk = jnp.concatenate([_pad_heads(wl[:, :fox_w], fox_heads, HEAD_DIM),
                               _pad_heads(wl[:, fox_w:2 * fox_w], fox_heads, HEAD_DIM)], axis=1).astype(BF16)
        wfvt = wl[:, 2 * fox_w:i1].T.astype(BF16)
        wff = jnp.pad(wl[:, i1:i2], ((0, 0), (0, LANES - fox_heads))).astype(BF16)
        fb = jnp.pad(fox_f_bias[l], (0, LANES - fox_heads)).reshape(1, LANES)
        wdqk = jnp.concatenate([_pad_maps(wl[:, i2:i2 + diff_w], diff_heads),
                                _pad_maps(wl[:, i2 + diff_w:i2 + 2 * diff_w], diff_heads)], axis=1).astype(BF16)
        wdvt = wl[:, i2 + 2 * diff_w:i3].T.astype(BF16)
        qa, ka, vft, q2, k2, vdt = _mixer_in(h, wqk, wfvt, wff, fb, wdqk, wdvt, tri, eqk, ones, ext2,
                                              fox_heads=fox_heads, diff_heads=diff_heads)

        bs_exp = jnp.repeat(jnp.transpose(sgu_b_s[l]), SGU_HEAD_DIM, axis=1)
        ys = _sgu(h.reshape(bsz * seq, d), wl[:, i3:i3 + sgu_w].astype(BF16), wl[:, i3 + sgu_w:].astype(BF16),
                  row(sgu_norm_g[l]), row(sgu_norm_b[l]), sgu_w_s[l], bs_exp).reshape(bsz, seq, sgu_w)

        yft = _fox_attention(qa, ka, vft)
        ydt = _diff_attention(q2, k2, vdt, row(diff_lambda_q1[l]), row(diff_lambda_k1[l]),
                              row(diff_lambda_q2[l]), row(diff_lambda_k2[l]),
                              diff_norm_g[l].reshape(diff_heads, DIFF_V_DIM, 1), lam_init=lam_init)

        h = _mixer_out(yft, ydt, ys, h, w_out[l].astype(BF16), row(norm_m_g[l]), row(norm_m_b[l]), alpha=alpha)

        h = _ffn_ln(h.reshape(bsz * seq, d), ffn_b_w_gate[l].astype(BF16), ffn_b_w_up[l].astype(BF16),
                    ffn_b_w_down[l].astype(BF16), row(norm_b_g[l]), row(norm_b_b[l]),
                    alpha=alpha).reshape(bsz, seq, d)
    return h
```

```python
import functools
import math

import numpy as np
import jax
import jax.numpy as jnp
from jax import lax
from jax.experimental import pallas as pl
from jax.experimental.pallas import tpu as pltpu

F32 = jnp.float32
BF16 = jnp.bfloat16

HEAD_DIM = 64
DIFF_QK_DIM = 32
DIFF_V_DIM = 64
SGU_HEAD_DIM = 64
CHUNK = 128
LN_EPS = 1e-5
NEG_INF = -1e30
LOG2E = math.log2(math.e)
DV_AUG = HEAD_DIM + 16

LANES = 128
VMEM_LIMIT = 56 * 1024 * 1024

FOX_EXT = HEAD_DIM
DIFF_EXT = DIFF_QK_DIM

TQ = 512
TK = 256
MXU_DIM = 256
TM_FFN = 1024
FFN_SUB = 512
TM_IN = 512
TM_SGU = 512
TM_OUT = 512


def _nt_dot(a, b):
    return lax.dot_general(a, b, (((1,), (1,)), ((), ())), preferred_element_type=F32)


def _dot(a, b):
    return jnp.dot(a, b, preferred_element_type=F32)


def _layer_norm(y, g, b):
    mu = jnp.mean(y, axis=-1, keepdims=True)
    d = y - mu
    var = jnp.mean(d * d, axis=-1, keepdims=True)
    return d * lax.rsqrt(var + LN_EPS) * g + b


def _split3(x):
    hi = x.astype(BF16)
    r1 = x - hi.astype(F32)
    mid = r1.astype(BF16)
    r2 = r1 - mid.astype(F32)
    lo = r2.astype(BF16)
    return hi, mid, lo


def _split3_np(x):
    x = np.asarray(x, np.float32)
    hi = x.astype(BF16).astype(np.float32)
    r1 = x - hi
    mid = r1.astype(BF16).astype(np.float32)
    lo = (r1 - mid).astype(BF16).astype(np.float32)
    return hi, mid, lo


def _ffn_kernel(x_ref, wg_ref, wu_ref, wd_ref, g_ref, b_ref, o_ref, *, alpha, sub, chunks):
    for r in range(x_ref.shape[0] // sub):
        rows = pl.ds(r * sub, sub)
        x = x_ref[rows, :]
        xb = x.astype(BF16)
        acc = None
        for c0, c1 in chunks:
            gate = _dot(xb, wg_ref[:, c0:c1])
            up = _dot(xb, wu_ref[:, c0:c1])
            hidden = (gate * jax.nn.sigmoid(gate) * up).astype(BF16)
            part = _dot(hidden, wd_ref[c0:c1, :])
            acc = part if acc is None else acc + part
        o_ref[rows, :] = _layer_norm(alpha * x + 0.5 * acc, g_ref[...], b_ref[...])


def _resident(shape):
    return pl.BlockSpec(shape, lambda *_: (0,) * len(shape), pipeline_mode=pl.Buffered(1))


def _ffn_ln(x2d, wg, wu, wd, g, b, *, alpha):
    t, d = x2d.shape
    f = wg.shape[1]
    tm = TM_FFN
    half = (f // MXU_DIM + 1) // 2 * MXU_DIM
    chunks = ((0, half), (half, f))
    return pl.pallas_call(
        functools.partial(_ffn_kernel, alpha=alpha, sub=FFN_SUB, chunks=chunks),
        grid=(t // tm,),
        in_specs=[
            pl.BlockSpec((tm, d), lambda i: (i, 0)),
            _resident(wg.shape), _resident(wu.shape), _resident(wd.shape),
            _resident(g.shape), _resident(b.shape),
        ],
        out_specs=pl.BlockSpec((tm, d), lambda i: (i, 0)),
        out_shape=jax.ShapeDtypeStruct((t, d), F32),
        compiler_params=pltpu.CompilerParams(
            dimension_semantics=("parallel",), vmem_limit_bytes=VMEM_LIMIT),
        name="ffn_ln",
    )(x2d, wg, wu, wd, g, b)


def _mixer_in_kernel(h_ref, wqk_ref, wfvt_ref, wff_ref, fb_ref, wd_ref, wdvt_ref, tri_ref,
                     eqk_ref, ones_ref, ext2_ref,
                     qa_ref, ka_ref, vft_ref, q2_ref, k2_ref, vdt_ref, carry_ref,
                     *, fox_heads, diff_heads, diff_scale, fox_scale):
    s = pl.program_id(1)
    tm = h_ref.shape[0]
    hb = h_ref[...].astype(BF16)

    @pl.when(s == 0)
    def _():
        carry_ref[...] = jnp.zeros_like(carry_ref)

    fx = _dot(hb, wff_ref[...]) + fb_ref[...]
    log_f = -(jnp.maximum(-fx, 0.0) + jnp.log1p(jnp.exp(-jnp.abs(fx))))
    lane = lax.broadcasted_iota(jnp.int32, log_f.shape, 1)
    log_f = jnp.where(lane < fox_heads, log_f, 0.0)
    t3 = jnp.concatenate(_split3(log_f), axis=1)
    cs = _dot(tri_ref[...], t3)
    c = cs[:, :LANES] + cs[:, LANES:2 * LANES] + cs[:, 2 * LANES:] + carry_ref[...]
    carry_ref[...] = c[tm - 1:tm, :]
    c3 = jnp.concatenate(_split3(c * LOG2E), axis=1)
    ext = _dot(c3, eqk_ref[...])

    nq = fox_heads * LANES
    pqk = _dot(hb, wqk_ref[...])
    for h in range(fox_heads):
        cq = slice(h * LANES, (h + 1) * LANES)
        ck = slice(nq + h * LANES, nq + (h + 1) * LANES)
        qa_ref[h] = (pqk[:, cq] * (fox_scale * LOG2E) + ext[:, :LANES] + ones_ref[:, cq]).astype(BF16)
        ka_ref[h] = (pqk[:, ck] + ext[:, LANES:] + ones_ref[:, ck]).astype(BF16)

    ones_rows = jnp.ones((DV_AUG - HEAD_DIM, TK), BF16)

    def store_values(vt, v_ref, heads):
        for cblk in range(tm // TK):
            for h in range(heads):
                v_ref[cblk, pl.ds(h * DV_AUG, HEAD_DIM), :] = (
                    vt[h * HEAD_DIM:(h + 1) * HEAD_DIM, cblk * TK:(cblk + 1) * TK])
                v_ref[cblk, pl.ds(h * DV_AUG + HEAD_DIM, DV_AUG - HEAD_DIM), :] = ones_rows

    store_values(_nt_dot(wfvt_ref[...], hb).astype(BF16), vft_ref, fox_heads)

    nd = diff_heads * LANES
    pd = _dot(hb, wd_ref[...])
    ext2 = ext2_ref[...].astype(F32)
    q2 = (pd[:, :nd] * (diff_scale * LOG2E) + ext2[:, :nd]).astype(BF16)
    k2 = (pd[:, nd:] + ext2[:, nd:]).astype(BF16)
    for h in range(diff_heads):
        q2_ref[h] = q2[:, h * LANES:(h + 1) * LANES]
        k2_ref[h] = k2[:, h * LANES:(h + 1) * LANES]

    store_values(_nt_dot(wdvt_ref[...], hb).astype(BF16), vdt_ref, diff_heads)


def _mixer_in(h, wqk, wfvt, wff, fb, wd, wdvt, tri, eqk, ones, ext2, *, fox_heads, diff_heads):
    b, s, d = h.shape
    tm = TM_IN
    nkv = s // TK
    fw = fox_heads * DV_AUG
    dw = diff_heads * DV_AUG
    const = lambda shape: pl.BlockSpec(shape, lambda bi, si: (0,) * len(shape))
    kern = functools.partial(
        _mixer_in_kernel, fox_heads=fox_heads, diff_heads=diff_heads,
        diff_scale=DIFF_QK_DIM ** -0.5, fox_scale=HEAD_DIM ** -0.5)
    return pl.pallas_call(
        kern,
        grid=(b, s // tm),
        in_specs=[
            pl.BlockSpec((None, tm, d), lambda bi, si: (bi, si, 0)),
            const(wqk.shape), const(wfvt.shape), const(wff.shape), const(fb.shape),
            const(wd.shape), const(wdvt.shape), const(tri.shape), const(eqk.shape),
            const(ones.shape),
            pl.BlockSpec((tm, ext2.shape[1]), lambda bi, si: (si, 0)),
        ],
        out_specs=[
            pl.BlockSpec((None, fox_heads, tm, LANES), lambda bi, si: (bi, 0, si, 0)),
            pl.BlockSpec((None, fox_heads, tm, LANES), lambda bi, si: (bi, 0, si, 0)),
            pl.BlockSpec((None, tm // TK, fw, TK), lambda bi, si: (bi, si, 0, 0)),
            pl.BlockSpec((None, diff_heads, tm, LANES), lambda bi, si: (bi, 0, si, 0)),
            pl.BlockSpec((None, diff_heads, tm, LANES), lambda bi, si: (bi, 0, si, 0)),
            pl.BlockSpec((None, tm // TK, dw, TK), lambda bi, si: (bi, si, 0, 0)),
        ],
        out_shape=[
            jax.ShapeDtypeStruct((b, fox_heads, s, LANES), BF16),
            jax.ShapeDtypeStruct((b, fox_heads, s, LANES), BF16),
            jax.ShapeDtypeStruct((b, nkv, fw, TK), BF16),
            jax.ShapeDtypeStruct((b, diff_heads, s, LANES), BF16),
            jax.ShapeDtypeStruct((b, diff_heads, s, LANES), BF16),
            jax.ShapeDtypeStruct((b, nkv, dw, TK), BF16),
        ],
        scratch_shapes=[pltpu.VMEM((1, LANES), F32)],
        compiler_params=pltpu.CompilerParams(
            dimension_semantics=("parallel", "arbitrary"), vmem_limit_bytes=VMEM_LIMIT),
        name="mixer_in",
    )(h, wqk, wfvt, wff, fb, wd, wdvt, tri, eqk, ones, ext2)


def _gelu_tanh(x):
    c = math.sqrt(2.0 / math.pi)
    return 0.5 * x * (1.0 + jnp.tanh(c * (x + 0.044715 * (x * x * x))))


def _sgu_kernel(h_ref, wu_ref, wg_ref, lg_ref, lb_ref, ws_ref, bs_ref, o_ref, *, heads):
    tm = h_ref.shape[0]
    hb = h_ref[...].astype(BF16)
    u = _gelu_tanh(_dot(hb, wu_ref[...]))
    g = _gelu_tanh(_dot(hb, wg_ref[...]))
    g = _layer_norm(g, lg_ref[...], lb_ref[...]).astype(BF16)
    row = lax.broadcasted_iota(jnp.int32, (CHUNK, CHUNK), 0)
    col = lax.broadcasted_iota(jnp.int32, (CHUNK, CHUNK), 1)
    causal = col <= row
    w = [jnp.where(causal, ws_ref[hh], 0.0).astype(BF16) for hh in range(heads)]
    lane = lax.broadcasted_iota(jnp.int32, (CHUNK, LANES), 1)
    low = lane < SGU_HEAD_DIM
    bias = bs_ref[...]
    for cblk in range(tm // CHUNK):
        rows = slice(cblk * CHUNK, (cblk + 1) * CHUNK)
        for p in range(heads // 2):
            cols = slice(p * LANES, (p + 1) * LANES)
            gp = g[rows, cols]
            zero = jnp.zeros_like(gp)
            mixed = (_dot(w[2 * p], jnp.where(low, gp, zero))
                     + _dot(w[2 * p + 1], jnp.where(low, zero, gp))
                     + bias[:, cols])
            o_ref[rows, cols] = (u[rows, cols] * mixed).astype(o_ref.dtype)


def _sgu(h2d, wu, wg, lg, lb, ws, bs_exp):
    t, d = h2d.shape
    width = wu.shape[1]
    heads = ws.shape[0]
    tm = TM_SGU
    const = lambda shape: pl.BlockSpec(shape, lambda i: (0,) * len(shape))
    return pl.pallas_call(
        functools.partial(_sgu_kernel, heads=heads),
        grid=(t // tm,),
        in_specs=[
            pl.BlockSpec((tm, d), lambda i: (i, 0)),
            const(wu.shape), const(wg.shape), const(lg.shape), const(lb.shape),
            const(ws.shape), const(bs_exp.shape),
        ],
        out_specs=pl.BlockSpec((tm, width), lambda i: (i, 0)),
        out_shape=jax.ShapeDtypeStruct((t, width), BF16),
        compiler_params=pltpu.CompilerParams(
            dimension_semantics=("parallel",), vmem_limit_bytes=VMEM_LIMIT),
        name="sgu",
    )(h2d, wu, wg, lg, lb, ws, bs_exp)


def _flash_sweep(qts, k_refs, v_refs, qi, scratch):
    n = len(qts)
    assert TQ == 2 * TK
    s_refs, bm_refs, m_ref, acc_ref, qt_ref = scratch[0:2], scratch[2:4], scratch[4], scratch[5], scratch[6]
    for i in range(n):
        qt_ref[i] = qts[i]

    def scores(i, j, slot):
        start = pl.multiple_of(j * TK, TK)
        st = _dot(k_refs[i][pl.ds(start, TK), :], qt_ref[i])
        s_refs[slot][i] = st
        bm_refs[slot][i] = jnp.max(st, axis=0, keepdims=True)

    def step(i, j, slot):
        scores(i, j + 1, 1 - slot)
        m = m_ref[i]
        m_new = jnp.maximum(m, bm_refs[slot][i])
        p = jnp.exp2(s_refs[slot][i] - m_new).astype(BF16)
        m_ref[i] = m_new
        acc_ref[i] = acc_ref[i] * jnp.exp2(m - m_new) + _dot(v_refs[i][j], p)

    for i in range(n):
        scores(i, 0, 0)
        m_ref[i] = jnp.full(m_ref.shape[1:], NEG_INF, F32)
        acc_ref[i] = jnp.zeros(acc_ref.shape[1:], F32)

    @pl.loop(0, qi)
    def _(u):
        for i in range(n):
            step(i, 2 * u, 0)
        for i in range(n):
            step(i, 2 * u + 1, 1)

    causal = (lax.broadcasted_iota(jnp.int32, (TK, TK), 0) <= lax.broadcasted_iota(jnp.int32, (TK, TK), 1))
    last = 2 * qi + 1
    for i in range(n):
        start = pl.multiple_of(last * TK, TK)
        s_refs[1][i, :, TK:] = _dot(k_refs[i][pl.ds(start, TK), :], qt_ref[i, :, TK:])
        st = s_refs[0][i]
        st = jnp.concatenate([jnp.where(causal, st[:, :TK], NEG_INF), st[:, TK:]], axis=1)
        m = m_ref[i]
        m_new = jnp.maximum(m, jnp.max(st, axis=0, keepdims=True))
        p = jnp.exp2(st - m_new).astype(BF16)
        m_ref[i] = m_new
        acc_ref[i] = acc_ref[i] * jnp.exp2(m - m_new) + _dot(v_refs[i][last - 1], p)
    for i in range(n):
        st = jnp.where(causal, s_refs[1][i, :, TK:], NEG_INF)
        m = m_ref[i, :, TK:]
        m_new = jnp.maximum(m, jnp.max(st, axis=0, keepdims=True))
        p = jnp.exp2(st - m_new).astype(BF16)
        acc_ref[i, :, TK:] = acc_ref[i, :, TK:] * jnp.exp2(m - m_new) + _dot(v_refs[i][last], p)


ACC_SCRATCH = 5


def _sweep_scratch(n):
    return ([pltpu.VMEM((n, TK, TQ), F32)] * 2 + [pltpu.VMEM((n, 1, TQ), F32)] * 2
            + [pltpu.VMEM((n, 1, TQ), F32), pltpu.VMEM((n, DV_AUG, TQ), F32),
               pltpu.VMEM((n, LANES, TQ), BF16)])


def _transposed(q):
    return jnp.transpose(q.astype(F32)).astype(BF16)


def _normalised(acc_ref, i):
    return acc_ref[i, :HEAD_DIM, :] / acc_ref[i, HEAD_DIM:HEAD_DIM + 1, :]


def _fox_kernel(q_ref, k_ref, v_ref, o_ref, *scratch, hp):
    qi = pl.program_id(2)
    qts = [_transposed(q_ref[hh]) for hh in range(hp)]
    k_refs = [k_ref.at[hh] for hh in range(hp)]
    v_refs = [v_ref.at[:, pl.ds(hh * DV_AUG, DV_AUG), :] for hh in range(hp)]
    _flash_sweep(qts, k_refs, v_refs, qi, scratch)
    for hh in range(hp):
        o_ref[pl.ds(hh * HEAD_DIM, HEAD_DIM), :] = _normalised(scratch[ACC_SCRATCH], hh).astype(o_ref.dtype)


def _fox_attention(qa, ka, vft, *, hp=6):
    b, heads, s, _ = qa.shape
    nkv = vft.shape[1]
    return pl.pallas_call(
        functools.partial(_fox_kernel, hp=hp),
        grid=(b, heads // hp, s // TQ),
        in_specs=[
            pl.BlockSpec((None, hp, TQ, LANES), lambda bi, hi, qi: (bi, hi, qi, 0)),
            pl.BlockSpec((None, hp, s, LANES), lambda bi, hi, qi: (bi, hi, 0, 0)),
            pl.BlockSpec((None, nkv, hp * DV_AUG, TK), lambda bi, hi, qi: (bi, 0, hi, 0)),
        ],
        out_specs=pl.BlockSpec((None, hp * HEAD_DIM, TQ), lambda bi, hi, qi: (bi, hi, qi)),
        out_shape=jax.ShapeDtypeStruct((b, heads * HEAD_DIM, s), BF16),
        scratch_shapes=_sweep_scratch(hp),
        compiler_params=pltpu.CompilerParams(
            dimension_semantics=("parallel", "parallel", "arbitrary"),
            vmem_limit_bytes=VMEM_LIMIT),
        name="fox_attention",
    )(qa, ka, vft)


def _diff_kernel(q_ref, k_ref, v_ref, lq1_ref, lk1_ref, lq2_ref, lk2_ref, g_ref, o_ref,
                 *scratch, hp, lam_init):
    qi = pl.program_id(2)
    lane = lax.broadcasted_iota(jnp.int32, (TQ, LANES), 1)
    first_map = lane < 2 * DIFF_QK_DIM
    qts, k_refs, v_refs = [], [], []
    for hh in range(hp):
        q2 = q_ref[hh]
        zero = jnp.zeros_like(q2)
        qts += [_transposed(jnp.where(first_map, q2, zero)), _transposed(jnp.where(first_map, zero, q2))]
        k_refs += [k_ref.at[hh]] * 2
        v_refs += [v_ref.at[:, pl.ds(hh * DV_AUG, DV_AUG), :]] * 2
    _flash_sweep(qts, k_refs, v_refs, qi, scratch)

    lam = (jnp.exp(jnp.sum(lq1_ref[...] * lk1_ref[...], axis=-1, keepdims=True))
           - jnp.exp(jnp.sum(lq2_ref[...] * lk2_ref[...], axis=-1, keepdims=True))
           + lam_init)
    for hh in range(hp):
        o = _normalised(scratch[ACC_SCRATCH], 2 * hh) - lam * _normalised(scratch[ACC_SCRATCH], 2 * hh + 1)
        ms = jnp.mean(o * o, axis=0, keepdims=True)
        o = o * lax.rsqrt(ms + LN_EPS)
        o = o * g_ref[hh] * (1.0 - lam_init)
        o_ref[pl.ds(hh * DIFF_V_DIM, DIFF_V_DIM), :] = o.astype(o_ref.dtype)


def _diff_attention(q2, k2, vdt, lq1, lk1, lq2, lk2, gcol, *, lam_init, hp=4):
    b, heads, s, _ = q2.shape
    nkv = vdt.shape[1]
    vec = lambda a: pl.BlockSpec(a.shape, lambda bi, hi, qi: (0, 0))
    return pl.pallas_call(
        functools.partial(_diff_kernel, hp=hp, lam_init=lam_init),
        grid=(b, heads // hp, s // TQ),
        in_specs=[
            pl.BlockSpec((None, hp, TQ, LANES), lambda bi, hi, qi: (bi, hi, qi, 0)),
            pl.BlockSpec((None, hp, s, LANES), lambda bi, hi, qi: (bi, hi, 0, 0)),
            pl.BlockSpec((None, nkv, hp * DV_AUG, TK), lambda bi, hi, qi: (bi, 0, hi, 0)),
            vec(lq1), vec(lk1), vec(lq2), vec(lk2),
            pl.BlockSpec((hp, DIFF_V_DIM, 1), lambda bi, hi, qi: (hi, 0, 0)),
        ],
        out_specs=pl.BlockSpec((None, hp * DIFF_V_DIM, TQ), lambda bi, hi, qi: (bi, hi, qi)),
        out_shape=jax.ShapeDtypeStruct((b, heads * DIFF_V_DIM, s), BF16),
        scratch_shapes=_sweep_scratch(2 * hp),
        compiler_params=pltpu.CompilerParams(
            dimension_semantics=("parallel", "parallel", "arbitrary"),
            vmem_limit_bytes=VMEM_LIMIT),
        name="diff_attention",
    )(q2, k2, vdt, lq1, lk1, lq2, lk2, gcol)


def _mixer_out_kernel(yf_ref, yd_ref, ys_ref, h_ref, w_ref, g_ref, b_ref, o_ref, *, alpha):
    yf = jnp.transpose(yf_ref[...].astype(F32)).astype(BF16)
    yd = jnp.transpose(yd_ref[...].astype(F32)).astype(BF16)
    y = jnp.concatenate([yf, yd, ys_ref[...]], axis=1)
    m = _dot(y, w_ref[...])
    o_ref[...] = _layer_norm(alpha * h_ref[...] + m, g_ref[...], b_ref[...])


def _mixer_out(yft, ydt, ys, h, w, g, b, *, alpha):
    bsz, s, d = h.shape
    fw, dw, sw = yft.shape[1], ydt.shape[1], ys.shape[2]
    tm = TM_OUT
    const = lambda shape: pl.BlockSpec(shape, lambda bi, si: (0,) * len(shape))
    return pl.pallas_call(
        functools.partial(_mixer_out_kernel, alpha=alpha),
        grid=(bsz, s // tm),
        in_specs=[
            pl.BlockSpec((None, fw, tm), lambda bi, si: (bi, 0, si)),
            pl.BlockSpec((None, dw, tm), lambda bi, si: (bi, 0, si)),
            pl.BlockSpec((None, tm, sw), lambda bi, si: (bi, si, 0)),
            pl.BlockSpec((None, tm, d), lambda bi, si: (bi, si, 0)),
            const(w.shape), const(g.shape), const(b.shape),
        ],
        out_specs=pl.BlockSpec((None, tm, d), lambda bi, si: (bi, si, 0)),
        out_shape=jax.ShapeDtypeStruct((bsz, s, d), F32),
        compiler_params=pltpu.CompilerParams(
            dimension_semantics=("parallel", "parallel"), vmem_limit_bytes=VMEM_LIMIT),
        name="mixer_out",
    )(yft, ydt, ys, h, w, g, b)


def _fox_bias_tables(fox_heads):
    nq = fox_heads * LANES
    nterm = 3 * fox_heads
    assert FOX_EXT + 2 * nterm <= LANES
    eqk = np.zeros((3 * LANES, 2 * LANES), np.float32)
    ones = np.zeros((1, 2 * nq), np.float32)
    for h in range(fox_heads):
        for t in range(3):
            a = FOX_EXT + fox_heads * t + h
            eqk[t * LANES + h, a + nterm] = 1.0
            eqk[t * LANES + h, LANES + a] = -1.0
            ones[0, h * LANES + a] = 1.0
            ones[0, nq + h * LANES + a + nterm] = 1.0
    return jnp.asarray(eqk, BF16), jnp.asarray(ones, F32)


def _alibi_tables(seq, diff_heads):
    nd = diff_heads * LANES
    ext = np.zeros((seq, 2 * nd), np.float32)
    pos = np.arange(seq, dtype=np.float32)
    slopes = np.float32(2.0) ** (np.float32(-8.0) * np.arange(1, diff_heads + 1, dtype=np.float32)
                                 / np.float32(diff_heads))
    for h in range(diff_heads):
        q_terms = _split3_np(np.float32(LOG2E) * (-slopes[h] * pos))
        k_terms = _split3_np(np.float32(LOG2E) * (slopes[h] * pos))
        for mp in range(2):
            base = h * LANES + mp * 2 * DIFF_QK_DIM + DIFF_EXT
            for t in range(3):
                ext[:, base + t] = 1.0
                ext[:, base + 3 + t] = q_terms[t]
                ext[:, nd + base + t] = k_terms[t]
                ext[:, nd + base + 3 + t] = 1.0
    return jnp.asarray(ext, BF16)


def _pad_heads(w, heads, dim):
    d = w.shape[0]
    return jnp.pad(w.reshape(d, heads, dim), ((0, 0), (0, 0), (0, LANES - dim))).reshape(d, heads * LANES)


def _pad_maps(w, heads):
    d = w.shape[0]
    w = w.reshape(d, heads, 2, DIFF_QK_DIM)
    w = jnp.pad(w, ((0, 0), (0, 0), (0, 0), (0, 2 * DIFF_QK_DIM - DIFF_QK_DIM)))
    return w.reshape(d, heads * LANES)


def kernel(x, ffn_a_w_gate, ffn_a_w_up, ffn_a_w_down, norm_a_g, norm_a_b, w_in, fox_f_bias, diff_lambda_q1, diff_lambda_k1, diff_lambda_q2, diff_lambda_k2, diff_norm_g, sgu_norm_g, sgu_norm_b, sgu_w_s, sgu_b_s, w_out, norm_m_g, norm_m_b, ffn_b_w_gate, ffn_b_w_up, ffn_b_w_down, norm_b_g, norm_b_b):
    bsz, seq, d = x.shape
    depth = w_in.shape[0]
    fox_heads = fox_f_bias.shape[1]
    fox_w = fox_heads * HEAD_DIM
    diff_w = diff_norm_g.shape[1]
    diff_heads = diff_w // DIFF_V_DIM
    sgu_w = sgu_norm_g.shape[1]
    sgu_heads = sgu_w_s.shape[1]
    alpha = (2 * depth) ** 0.25

    eqk, ones = _fox_bias_tables(fox_heads)
    ext2 = _alibi_tables(seq, diff_heads)
    tri = jnp.asarray(np.tril(np.ones((TM_IN, TM_IN), np.float32)), BF16)

    row = lambda v: v.reshape(1, -1)
    h = x
    for l in range(depth):
        lam_init = 0.8 - 0.6 * math.exp(-0.3 * l)
        h = _ffn_ln(h.reshape(bsz * seq, d), ffn_a_w_gate[l].astype(BF16), ffn_a_w_up[l].astype(BF16),
                    ffn_a_w_down[l].astype(BF16), row(norm_a_g[l]), row(norm_a_b[l]),
                    alpha=alpha).reshape(bsz, seq, d)

        wl = w_in[l]
        i1 = 3 * fox_w
        i2 = i1 + fox_heads
        i3 = i2 + 3 * diff_w
        wqk = jnp.concatenate([_pad_heads(wl[:, :fox_w], fox_heads, HEAD_DIM),
                               _pad_heads(wl[:, fox_w:2 * fox_w], fox_heads, HEAD_DIM)], axis=1).astype(BF16)
        wfvt = wl[:, 2 * fox_w:i1].T.astype(BF16)
        wff = jnp.pad(wl[:, i1:i2], ((0, 0), (0, LANES - fox_heads))).astype(BF16)
        fb = jnp.pad(fox_f_bias[l], (0, LANES - fox_heads)).reshape(1, LANES)
        wdqk = jnp.concatenate([_pad_maps(wl[:, i2:i2 + diff_w], diff_heads),
                                _pad_maps(wl[:, i2 + diff_w:i2 + 2 * diff_w], diff_heads)], axis=1).astype(BF16)
        wdvt = wl[:, i2 + 2 * diff_w:i3].T.astype(BF16)
        qa, ka, vft, q2, k2, vdt = _mixer_in(h, wqk, wfvt, wff, fb, wdqk, wdvt, tri, eqk, ones, ext2,
                                              fox_heads=fox_heads, diff_heads=diff_heads)

        bs_exp = jnp.repeat(jnp.transpose(sgu_b_s[l]), SGU_HEAD_DIM, axis=1)
        ys = _sgu(h.reshape(bsz * seq, d), wl[:, i3:i3 + sgu_w].astype(BF16), wl[:, i3 + sgu_w:].astype(BF16),
                  row(sgu_norm_g[l]), row(sgu_norm_b[l]), sgu_w_s[l], bs_exp).reshape(bsz, seq, sgu_w)

        yft = _fox_attention(qa, ka, vft)
        ydt = _diff_attention(q2, k2, vdt, row(diff_lambda_q1[l]), row(diff_lambda_k1[l]),
                              row(diff_lambda_q2[l]), row(diff_lambda_k2[l]),
                              diff_norm_g[l].reshape(diff_heads, DIFF_V_DIM, 1), lam_init=lam_init)

        h = _mixer_out(yft, ydt, ys, h, w_out[l].astype(BF16), row(norm_m_g[l]), row(norm_m_b[l]), alpha=alpha)

        h = _ffn_ln(h.reshape(bsz * seq, d), ffn_b_w_gate[l].astype(BF16), ffn_b_w_up[l].astype(BF16),
                    ffn_b_w_down[l].astype(BF16), row(norm_b_g[l]), row(norm_b_b[l]),
                    alpha=alpha).reshape(bsz, seq, d)
    return h
```

```python
import functools
import math

import numpy as np
import jax
import jax.numpy as jnp
from jax import lax
from jax.experimental import pallas as pl
from jax.experimental.pallas import tpu as pltpu

F32 = jnp.float32
BF16 = jnp.bfloat16

HEAD_DIM = 64
DIFF_QK_DIM = 32
DIFF_V_DIM = 64
SGU_HEAD_DIM = 64
CHUNK = 128
LN_EPS = 1e-5
NEG_INF = -1e30
LOG2E = math.log2(math.e)
DV_AUG = HEAD_DIM + 16

LANES = 128
VMEM_LIMIT = 56 * 1024 * 1024

FOX_EXT = HEAD_DIM
DIFF_EXT = DIFF_QK_DIM

TQ = 512
TK = 256
MXU_DIM = 256
TM_FFN = 1024
FFN_SUB = 512
TM_IN = 512


def _nt_dot(a, b):
    return lax.dot_general(a, b, (((1,), (1,)), ((), ())), preferred_element_type=F32)


def _dot(a, b):
    return jnp.dot(a, b, preferred_element_type=F32)


def _layer_norm(y, g, b):
    mu = jnp.mean(y, axis=-1, keepdims=True)
    d = y - mu
    var = jnp.mean(d * d, axis=-1, keepdims=True)
    return d * lax.rsqrt(var + LN_EPS) * g + b


def _split3(x):
    hi = x.astype(BF16)
    r1 = x - hi.astype(F32)
    mid = r1.astype(BF16)
    r2 = r1 - mid.astype(F32)
    lo = r2.astype(BF16)
    return hi, mid, lo


def _split3_np(x):
    x = np.asarray(x, np.float32)
    hi = x.astype(BF16).astype(np.float32)
    r1 = x - hi
    mid = r1.astype(BF16).astype(np.float32)
    lo = (r1 - mid).astype(BF16).astype(np.float32)
    return hi, mid, lo


def _ffn_groups(load_rows, n_groups, sub, wg_ref, wu_ref, wd_ref, g_ref, b_ref, o_ref, *, alpha, chunks):
    x = load_rows(pl.ds(0, sub))
    finish = None
    for r in range(n_groups):
        xb = x.astype(BF16)
        acc = None
        x_next = None
        for ci, (c0, c1) in enumerate(chunks):
            gate = _dot(xb, wg_ref[:, c0:c1])
            up = _dot(xb, wu_ref[:, c0:c1])
            hidden = (gate * jax.nn.sigmoid(gate) * up).astype(BF16)
            part = _dot(hidden, wd_ref[c0:c1, :])
            acc = part if acc is None else acc + part
            if ci == 0:
                if finish is not None:
                    finish()
                if r + 1 < n_groups:
                    x_next = load_rows(pl.ds((r + 1) * sub, sub))
        y = alpha * x + 0.5 * acc

        def finish(y=y, rows=pl.ds(r * sub, sub)):
            o_ref[rows, :] = _layer_norm(y, g_ref[...], b_ref[...])
        x = x_next
    finish()


def _ffn_kernel(x_ref, wg_ref, wu_ref, wd_ref, g_ref, b_ref, o_ref, *, alpha, sub, chunks):
    _ffn_groups(lambda rows: x_ref[rows, :], x_ref.shape[0] // sub, sub,
                wg_ref, wu_ref, wd_ref, g_ref, b_ref, o_ref, alpha=alpha, chunks=chunks)


def _out_ffn_kernel(yf_ref, yd_ref, ys_ref, h_ref, wo_ref, gm_ref, bm_ref,
                    wg_ref, wu_ref, wd_ref, g_ref, b_ref, o_ref, *, alpha, sub, chunks):
    def mixed_rows(rows):
        yf = jnp.transpose(yf_ref[:, rows].astype(F32)).astype(BF16)
        yd = jnp.transpose(yd_ref[:, rows].astype(F32)).astype(BF16)
        y = jnp.concatenate([yf, yd, ys_ref[rows, :]], axis=1)
        return _layer_norm(alpha * h_ref[rows, :] + _dot(y, wo_ref[...]), gm_ref[...], bm_ref[...])

    _ffn_groups(mixed_rows, h_ref.shape[0] // sub, sub,
                wg_ref, wu_ref, wd_ref, g_ref, b_ref, o_ref, alpha=alpha, chunks=chunks)


def _resident(shape):
    return pl.BlockSpec(shape, lambda *_: (0,) * len(shape), pipeline_mode=pl.Buffered(1))


def _hidden_chunks(f):
    half = (f // MXU_DIM + 1) // 2 * MXU_DIM
    return ((0, half), (half, f))


def _out_ffn(yft, ydt, ys, h, wo, gm, bm, wg, wu, wd, g, b, *, alpha):
    bsz, s, d = h.shape
    fw, dw, sw = yft.shape[1], ydt.shape[1], ys.shape[2]
    tm = TM_FFN
    kern = functools.partial(_out_ffn_kernel, alpha=alpha, sub=FFN_SUB, chunks=_hidden_chunks(wg.shape[1]))
    return pl.pallas_call(
        kern,
        grid=(bsz, s // tm),
        in_specs=[
            pl.BlockSpec((None, fw, tm), lambda bi, si: (bi, 0, si)),
            pl.BlockSpec((None, dw, tm), lambda bi, si: (bi, 0, si)),
            pl.BlockSpec((None, tm, sw), lambda bi, si: (bi, si, 0)),
            pl.BlockSpec((None, tm, d), lambda bi, si: (bi, si, 0)),
            _resident(wo.shape), _resident(gm.shape), _resident(bm.shape),
            _resident(wg.shape), _resident(wu.shape), _resident(wd.shape),
            _resident(g.shape), _resident(b.shape),
        ],
        out_specs=pl.BlockSpec((None, tm, d), lambda bi, si: (bi, si, 0)),
        out_shape=jax.ShapeDtypeStruct((bsz, s, d), F32),
        compiler_params=pltpu.CompilerParams(
            dimension_semantics=("parallel", "parallel"), vmem_limit_bytes=VMEM_LIMIT),
        name="out_ffn",
    )(yft, ydt, ys, h, wo, gm, bm, wg, wu, wd, g, b)


def _ffn_ln(x2d, wg, wu, wd, g, b, *, alpha):
    t, d = x2d.shape
    tm = TM_FFN
    chunks = _hidden_chunks(wg.shape[1])
    return pl.pallas_call(
        functools.partial(_ffn_kernel, alpha=alpha, sub=FFN_SUB, chunks=chunks),
        grid=(t // tm,),
        in_specs=[
            pl.BlockSpec((tm, d), lambda i: (i, 0)),
            _resident(wg.shape), _resident(wu.shape), _resident(wd.shape),
            _resident(g.shape), _resident(b.shape),
        ],
        out_specs=pl.BlockSpec((tm, d), lambda i: (i, 0)),
        out_shape=jax.ShapeDtypeStruct((t, d), F32),
        compiler_params=pltpu.CompilerParams(
            dimension_semantics=("parallel",), vmem_limit_bytes=VMEM_LIMIT),
        name="ffn_ln",
    )(x2d, wg, wu, wd, g, b)


def _mixer_in_kernel(h_ref, wqk_ref, wfvt_ref, wff_ref, fb_ref, wd_ref, wdvt_ref, tri_ref,
                     eqk_ref, ones_ref, ext2_ref, wsu_ref, wsg_ref, lg_ref, lb_ref, ws_ref, bs_ref,
                     qa_ref, ka_ref, vft_ref, q2_ref, k2_ref, vdt_ref, ys_ref, carry_ref,
                     *, fox_heads, diff_heads, diff_scale, fox_scale):
    s = pl.program_id(1)
    tm = h_ref.shape[0]
    hb = h_ref[...].astype(BF16)

    @pl.when(s == 0)
    def _():
        carry_ref[...] = jnp.zeros_like(carry_ref)

    fx = _dot(hb, wff_ref[...]) + fb_ref[...]
    nq = fox_heads * LANES
    pqk = _dot(hb, wqk_ref[...])

    log_f = -(jnp.maximum(-fx, 0.0) + jnp.log1p(jnp.exp(-jnp.abs(fx))))
    lane = lax.broadcasted_iota(jnp.int32, log_f.shape, 1)
    log_f = jnp.where(lane < fox_heads, log_f, 0.0)
    t3 = jnp.concatenate(_split3(log_f), axis=1)
    cs = _dot(tri_ref[...], t3)
    c = cs[:, :LANES] + cs[:, LANES:2 * LANES] + cs[:, 2 * LANES:] + carry_ref[...]
    carry_ref[...] = c[tm - 1:tm, :]
    c3 = jnp.concatenate(_split3(c * LOG2E), axis=1)
    ext = _dot(c3, eqk_ref[...])

    su = _dot(hb, wsu_ref[...])
    sg = _dot(hb, wsg_ref[...])

    for h in range(fox_heads):
        cq = slice(h * LANES, (h + 1) * LANES)
        ck = slice(nq + h * LANES, nq + (h + 1) * LANES)
        qa_ref[h] = (pqk[:, cq] * (fox_scale * LOG2E) + ext[:, :LANES] + ones_ref[:, cq]).astype(BF16)
        ka_ref[h] = (pqk[:, ck] + ext[:, LANES:] + ones_ref[:, ck]).astype(BF16)

    ones_rows = jnp.ones((DV_AUG - HEAD_DIM, TK), BF16)

    def store_values(vt, v_ref, heads):
        for cblk in range(tm // TK):
            for h in range(heads):
                v_ref[cblk, pl.ds(h * DV_AUG, HEAD_DIM), :] = (
                    vt[h * HEAD_DIM:(h + 1) * HEAD_DIM, cblk * TK:(cblk + 1) * TK])
                v_ref[cblk, pl.ds(h * DV_AUG + HEAD_DIM, DV_AUG - HEAD_DIM), :] = ones_rows

    store_values(_nt_dot(wfvt_ref[...], hb).astype(BF16), vft_ref, fox_heads)

    u = _gelu_tanh(su)
    g = _layer_norm(_gelu_tanh(sg), lg_ref[...], lb_ref[...]).astype(BF16)

    nd = diff_heads * LANES
    pd = _dot(hb, wd_ref[...])

    _sgu_mix(u, g, ws_ref, bs_ref, ys_ref)

    ext2 = ext2_ref[...].astype(F32)
    q2 = (pd[:, :nd] * (diff_scale * LOG2E) + ext2[:, :nd]).astype(BF16)
    k2 = (pd[:, nd:] + ext2[:, nd:]).astype(BF16)
    for h in range(diff_heads):
        q2_ref[h] = q2[:, h * LANES:(h + 1) * LANES]
        k2_ref[h] = k2[:, h * LANES:(h + 1) * LANES]

    store_values(_nt_dot(wdvt_ref[...], hb).astype(BF16), vdt_ref, diff_heads)


def _mixer_in(h, wqk, wfvt, wff, fb, wd, wdvt, tri, eqk, ones, ext2, wsu, wsg, lg, lb, ws, bs_exp,
              *, fox_heads, diff_heads):
    b, s, d = h.shape
    tm = TM_IN
    nkv = s // TK
    fw = fox_heads * DV_AUG
    dw = diff_heads * DV_AUG
    const = _resident
    kern = functools.partial(
        _mixer_in_kernel, fox_heads=fox_heads, diff_heads=diff_heads,
        diff_scale=DIFF_QK_DIM ** -0.5, fox_scale=HEAD_DIM ** -0.5)
    return pl.pallas_call(
        kern,
        grid=(b, s // tm),
        in_specs=[
            pl.BlockSpec((None, tm, d), lambda bi, si: (bi, si, 0)),
            const(wqk.shape), const(wfvt.shape), const(wff.shape), const(fb.shape),
            const(wd.shape), const(wdvt.shape), const(tri.shape), const(eqk.shape),
            const(ones.shape),
            pl.BlockSpec((tm, ext2.shape[1]), lambda bi, si: (si, 0)),
            const(wsu.shape), const(wsg.shape), const(lg.shape), const(lb.shape),
            const(ws.shape), const(bs_exp.shape),
        ],
        out_specs=[
            pl.BlockSpec((None, fox_heads, tm, LANES), lambda bi, si: (bi, 0, si, 0)),
            pl.BlockSpec((None, fox_heads, tm, LANES), lambda bi, si: (bi, 0, si, 0)),
            pl.BlockSpec((None, tm // TK, fw, TK), lambda bi, si: (bi, si, 0, 0)),
            pl.BlockSpec((None, diff_heads, tm, LANES), lambda bi, si: (bi, 0, si, 0)),
            pl.BlockSpec((None, diff_heads, tm, LANES), lambda bi, si: (bi, 0, si, 0)),
            pl.BlockSpec((None, tm // TK, dw, TK), lambda bi, si: (bi, si, 0, 0)),
            pl.BlockSpec((None, tm, wsu.shape[1]), lambda bi, si: (bi, si, 0)),
        ],
        out_shape=[
            jax.ShapeDtypeStruct((b, fox_heads, s, LANES), BF16),
            jax.ShapeDtypeStruct((b, fox_heads, s, LANES), BF16),
            jax.ShapeDtypeStruct((b, nkv, fw, TK), BF16),
            jax.ShapeDtypeStruct((b, diff_heads, s, LANES), BF16),
            jax.ShapeDtypeStruct((b, diff_heads, s, LANES), BF16),
            jax.ShapeDtypeStruct((b, nkv, dw, TK), BF16),
            jax.ShapeDtypeStruct((b, s, wsu.shape[1]), BF16),
        ],
        scratch_shapes=[pltpu.VMEM((1, LANES), F32)],
        compiler_params=pltpu.CompilerParams(
            dimension_semantics=("parallel", "arbitrary"), vmem_limit_bytes=VMEM_LIMIT),
        name="mixer_in",
    )(h, wqk, wfvt, wff, fb, wd, wdvt, tri, eqk, ones, ext2, wsu, wsg, lg, lb, ws, bs_exp)


def _gelu_tanh(x):
    c = math.sqrt(2.0 / math.pi)
    return 0.5 * x * (1.0 + jnp.tanh(c * (x + 0.044715 * (x * x * x))))


def _sgu_mix(u, g, ws_ref, bs_ref, o_ref):
    tm = u.shape[0]
    heads = ws_ref.shape[0]
    row = lax.broadcasted_iota(jnp.int32, (CHUNK, CHUNK), 0)
    col = lax.broadcasted_iota(jnp.int32, (CHUNK, CHUNK), 1)
    causal = col <= row
    w = [jnp.where(causal, ws_ref[hh], 0.0).astype(BF16) for hh in range(heads)]
    lane = lax.broadcasted_iota(jnp.int32, (CHUNK, LANES), 1)
    low = lane < SGU_HEAD_DIM
    bias = bs_ref[...]
    for cblk in range(tm // CHUNK):
        rows = slice(cblk * CHUNK, (cblk + 1) * CHUNK)
        for p in range(heads // 2):
            cols = slice(p * LANES, (p + 1) * LANES)
            gp = g[rows, cols]
            zero = jnp.zeros_like(gp)
            mixed = (_dot(w[2 * p], jnp.where(low, gp, zero))
                     + _dot(w[2 * p + 1], jnp.where(low, zero, gp))
                     + bias[:, cols])
            o_ref[rows, cols] = (u[rows, cols] * mixed).astype(o_ref.dtype)


def _flash_sweep(qts, k_refs, v_refs, qi, scratch):
    n = len(qts)
    assert TQ == 2 * TK
    s_refs, bm_refs, m_ref, acc_ref, qt_ref = scratch[0:2], scratch[2:4], scratch[4], scratch[5], scratch[6]
    for i in range(n):
        qt_ref[i] = qts[i]

    def scores(i, j, slot):
        start = pl.multiple_of(j * TK, TK)
        st = _dot(k_refs[i][pl.ds(start, TK), :], qt_ref[i])
        s_refs[slot][i] = st
        bm_refs[slot][i] = jnp.max(st, axis=0, keepdims=True)

    def step(i, j, slot):
        scores(i, j + 1, 1 - slot)
        m = m_ref[i]
        m_new = jnp.maximum(m, bm_refs[slot][i])
        p = jnp.exp2(s_refs[slot][i] - m_new).astype(BF16)
        m_ref[i] = m_new
        acc_ref[i] = acc_ref[i] * jnp.exp2(m - m_new) + _dot(v_refs[i][j], p)

    for i in range(n):
        scores(i, 0, 0)
        m_ref[i] = jnp.full(m_ref.shape[1:], NEG_INF, F32)
        acc_ref[i] = jnp.zeros(acc_ref.shape[1:], F32)

    @pl.loop(0, qi)
    def _(u):
        for i in range(n):
            step(i, 2 * u, 0)
        for i in range(n):
            step(i, 2 * u + 1, 1)

    causal = (lax.broadcasted_iota(jnp.int32, (TK, TK), 0) <= lax.broadcasted_iota(jnp.int32, (TK, TK), 1))
    last = 2 * qi + 1
    for i in range(n):
        start = pl.multiple_of(last * TK, TK)
        s_refs[1][i, :, TK:] = _dot(k_refs[i][pl.ds(start, TK), :], qt_ref[i, :, TK:])
        st = s_refs[0][i]
        st = jnp.concatenate([jnp.where(causal, st[:, :TK], NEG_INF), st[:, TK:]], axis=1)
        m = m_ref[i]
        m_new = jnp.maximum(m, jnp.max(st, axis=0, keepdims=True))
        p = jnp.exp2(st - m_new).astype(BF16)
        m_ref[i] = m_new
        acc_ref[i] = acc_ref[i] * jnp.exp2(m - m_new) + _dot(v_refs[i][last - 1], p)
    for i in range(n):
        st = jnp.where(causal, s_refs[1][i, :, TK:], NEG_INF)
        m = m_ref[i, :, TK:]
        m_new = jnp.maximum(m, jnp.max(st, axis=0, keepdims=True))
        p = jnp.exp2(st - m_new).astype(BF16)
        acc_ref[i, :, TK:] = acc_ref[i, :, TK:] * jnp.exp2(m - m_new) + _dot(v_refs[i][last], p)


ACC_SCRATCH = 5


def _sweep_scratch(n):
    return ([pltpu.VMEM((n, TK, TQ), F32)] * 2 + [pltpu.VMEM((n, 1, TQ), F32)] * 2
            + [pltpu.VMEM((n, 1, TQ), F32), pltpu.VMEM((n, DV_AUG, TQ), F32),
               pltpu.VMEM((n, LANES, TQ), BF16)])


def _transposed(q):
    return jnp.transpose(q.astype(F32)).astype(BF16)


def _normalised(acc_ref, i):
    return acc_ref[i, :HEAD_DIM, :] / acc_ref[i, HEAD_DIM:HEAD_DIM + 1, :]


def _fox_kernel(q_ref, k_ref, v_ref, o_ref, *scratch, hp):
    qi = pl.program_id(2)
    qts = [_transposed(q_ref[hh]) for hh in range(hp)]
    k_refs = [k_ref.at[hh] for hh in range(hp)]
    v_refs = [v_ref.at[:, pl.ds(hh * DV_AUG, DV_AUG), :] for hh in range(hp)]
    _flash_sweep(qts, k_refs, v_refs, qi, scratch)
    for hh in range(hp):
        o_ref[pl.ds(hh * HEAD_DIM, HEAD_DIM), :] = _normalised(scratch[ACC_SCRATCH], hh).astype(o_ref.dtype)


def _fox_attention(qa, ka, vft, *, hp=6):
    b, heads, s, _ = qa.shape
    nkv = vft.shape[1]
    return pl.pallas_call(
        functools.partial(_fox_kernel, hp=hp),
        grid=(b, heads // hp, s // TQ),
        in_specs=[
            pl.BlockSpec((None, hp, TQ, LANES), lambda bi, hi, qi: (bi, hi, qi, 0)),
            pl.BlockSpec((None, hp, s, LANES), lambda bi, hi, qi: (bi, hi, 0, 0)),
            pl.BlockSpec((None, nkv, hp * DV_AUG, TK), lambda bi, hi, qi: (bi, 0, hi, 0)),
        ],
        out_specs=pl.BlockSpec((None, hp * HEAD_DIM, TQ), lambda bi, hi, qi: (bi, hi, qi)),
        out_shape=jax.ShapeDtypeStruct((b, heads * HEAD_DIM, s), BF16),
        scratch_shapes=_sweep_scratch(hp),
        compiler_params=pltpu.CompilerParams(
            dimension_semantics=("parallel", "parallel", "arbitrary"),
            vmem_limit_bytes=VMEM_LIMIT),
        name="fox_attention",
    )(qa, ka, vft)


def _diff_kernel(q_ref, k_ref, v_ref, lq1_ref, lk1_ref, lq2_ref, lk2_ref, g_ref, o_ref,
                 *scratch, hp, lam_init):
    qi = pl.program_id(2)
    lane = lax.broadcasted_iota(jnp.int32, (TQ, LANES), 1)
    first_map = lane < 2 * DIFF_QK_DIM
    qts, k_refs, v_refs = [], [], []
    for hh in range(hp):
        q2 = q_ref[hh]
        zero = jnp.zeros_like(q2)
        qts += [_transposed(jnp.where(first_map, q2, zero)), _transposed(jnp.where(first_map, zero, q2))]
        k_refs += [k_ref.at[hh]] * 2
        v_refs += [v_ref.at[:, pl.ds(hh * DV_AUG, DV_AUG), :]] * 2
    _flash_sweep(qts, k_refs, v_refs, qi, scratch)

    lam = (jnp.exp(jnp.sum(lq1_ref[...] * lk1_ref[...], axis=-1, keepdims=True))
           - jnp.exp(jnp.sum(lq2_ref[...] * lk2_ref[...], axis=-1, keepdims=True))
           + lam_init)
    for hh in range(hp):
        o = _normalised(scratch[ACC_SCRATCH], 2 * hh) - lam * _normalised(scratch[ACC_SCRATCH], 2 * hh + 1)
        ms = jnp.mean(o * o, axis=0, keepdims=True)
        o = o * lax.rsqrt(ms + LN_EPS)
        o = o * g_ref[hh] * (1.0 - lam_init)
        o_ref[pl.ds(hh * DIFF_V_DIM, DIFF_V_DIM), :] = o.astype(o_ref.dtype)


def _diff_attention(q2, k2, vdt, lq1, lk1, lq2, lk2, gcol, *, lam_init, hp=4):
    b, heads, s, _ = q2.shape
    nkv = vdt.shape[1]
    vec = lambda a: pl.BlockSpec(a.shape, lambda bi, hi, qi: (0, 0))
    return pl.pallas_call(
        functools.partial(_diff_kernel, hp=hp, lam_init=lam_init),
        grid=(b, heads // hp, s // TQ),
        in_specs=[
            pl.BlockSpec((None, hp, TQ, LANES), lambda bi, hi, qi: (bi, hi, qi, 0)),
            pl.BlockSpec((None, hp, s, LANES), lambda bi, hi, qi: (bi, hi, 0, 0)),
            pl.BlockSpec((None, nkv, hp * DV_AUG, TK), lambda bi, hi, qi: (bi, 0, hi, 0)),
            vec(lq1), vec(lk1), vec(lq2), vec(lk2),
            pl.BlockSpec((hp, DIFF_V_DIM, 1), lambda bi, hi, qi: (hi, 0, 0)),
        ],
        out_specs=pl.BlockSpec((None, hp * DIFF_V_DIM, TQ), lambda bi, hi, qi: (bi, hi, qi)),
        out_shape=jax.ShapeDtypeStruct((b, heads * DIFF_V_DIM, s), BF16),
        scratch_shapes=_sweep_scratch(2 * hp),
        compiler_params=pltpu.CompilerParams(
            dimension_semantics=("parallel", "parallel", "arbitrary"),
            vmem_limit_bytes=VMEM_LIMIT),
        name="diff_attention",
    )(q2, k2, vdt, lq1, lk1, lq2, lk2, gcol)


def _fox_bias_tables(fox_heads):
    nq = fox_heads * LANES
    nterm = 3 * fox_heads
    assert FOX_EXT + 2 * nterm <= LANES
    eqk = np.zeros((3 * LANES, 2 * LANES), np.float32)
    ones = np.zeros((1, 2 * nq), np.float32)
    for h in range(fox_heads):
        for t in range(3):
            a = FOX_EXT + fox_heads * t + h
            eqk[t * LANES + h, a + nterm] = 1.0
            eqk[t * LANES + h, LANES + a] = -1.0
            ones[0, h * LANES + a] = 1.0
            ones[0, nq + h * LANES + a + nterm] = 1.0
    return jnp.asarray(eqk, BF16), jnp.asarray(ones, F32)


def _alibi_tables(seq, diff_heads):
    nd = diff_heads * LANES
    ext = np.zeros((seq, 2 * nd), np.float32)
    pos = np.arange(seq, dtype=np.float32)
    slopes = np.float32(2.0) ** (np.float32(-8.0) * np.arange(1, diff_heads + 1, dtype=np.float32)
                                 / np.float32(diff_heads))
    for h in range(diff_heads):
        q_terms = _split3_np(np.float32(LOG2E) * (-slopes[h] * pos))
        k_terms = _split3_np(np.float32(LOG2E) * (slopes[h] * pos))
        for mp in range(2):
            base = h * LANES + mp * 2 * DIFF_QK_DIM + DIFF_EXT
            for t in range(3):
                ext[:, base + t] = 1.0
                ext[:, base + 3 + t] = q_terms[t]
                ext[:, nd + base + t] = k_terms[t]
                ext[:, nd + base + 3 + t] = 1.0
    return jnp.asarray(ext, BF16)


def _pad_heads(w, heads, dim):
    d = w.shape[0]
    return jnp.pad(w.reshape(d, heads, dim), ((0, 0), (0, 0), (0, LANES - dim))).reshape(d, heads * LANES)


def _pad_maps(w, heads):
    d = w.shape[0]
    w = w.reshape(d, heads, 2, DIFF_QK_DIM)
    w = jnp.pad(w, ((0, 0), (0, 0), (0, 0), (0, 2 * DIFF_QK_DIM - DIFF_QK_DIM)))
    return w.reshape(d, heads * LANES)


def kernel(x, ffn_a_w_gate, ffn_a_w_up, ffn_a_w_down, norm_a_g, norm_a_b, w_in, fox_f_bias, diff_lambda_q1, diff_lambda_k1, diff_lambda_q2, diff_lambda_k2, diff_norm_g, sgu_norm_g, sgu_norm_b, sgu_w_s, sgu_b_s, w_out, norm_m_g, norm_m_b, ffn_b_w_gate, ffn_b_w_up, ffn_b_w_down, norm_b_g, norm_b_b):
    bsz, seq, d = x.shape
    depth = w_in.shape[0]
    fox_heads = fox_f_bias.shape[1]
    fox_w = fox_heads * HEAD_DIM
    diff_w = diff_norm_g.shape[1]
    diff_heads = diff_w // DIFF_V_DIM
    sgu_w = sgu_norm_g.shape[1]
    sgu_heads = sgu_w_s.shape[1]
    alpha = (2 * depth) ** 0.25

    eqk, ones = _fox_bias_tables(fox_heads)
    ext2 = _alibi_tables(seq, diff_heads)
    tri = jnp.asarray(np.tril(np.ones((TM_IN, TM_IN), np.float32)), BF16)

    row = lambda v: v.reshape(1, -1)
    h = x
    for l in range(depth):
        lam_init = 0.8 - 0.6 * math.exp(-0.3 * l)
        h = _ffn_ln(h.reshape(bsz * seq, d), ffn_a_w_gate[l].astype(BF16), ffn_a_w_up[l].astype(BF16),
                    ffn_a_w_down[l].astype(BF16), row(norm_a_g[l]), row(norm_a_b[l]),
                    alpha=alpha).reshape(bsz, seq, d)

        wl = w_in[l]
        i1 = 3 * fox_w
        i2 = i1 + fox_heads
        i3 = i2 + 3 * diff_w
        wqk = jnp.concatenate([_pad_heads(wl[:, :fox_w], fox_heads, HEAD_DIM),
                               _pad_heads(wl[:, fox_w:2 * fox_w], fox_heads, HEAD_DIM)], axis=1).astype(BF16)
        wfvt = wl[:, 2 * fox_w:i1].T.astype(BF16)
        wff = jnp.pad(wl[:, i1:i2], ((0, 0), (0, LANES - fox_heads))).astype(BF16)
        fb = jnp.pad(fox_f_bias[l], (0, LANES - fox_heads)).reshape(1, LANES)
        wdqk = jnp.concatenate([_pad_maps(wl[:, i2:i2 + diff_w], diff_heads),
                                _pad_maps(wl[:, i2 + diff_w:i2 + 2 * diff_w], diff_heads)], axis=1).astype(BF16)
        wdvt = wl[:, i2 + 2 * diff_w:i3].T.astype(BF16)
        bs_exp = jnp.repeat(jnp.transpose(sgu_b_s[l]), SGU_HEAD_DIM, axis=1)
        qa, ka, vft, q2, k2, vdt, ys = _mixer_in(
            h, wqk, wfvt, wff, fb, wdqk, wdvt, tri, eqk, ones, ext2,
            wl[:, i3:i3 + sgu_w].astype(BF16), wl[:, i3 + sgu_w:].astype(BF16),
            row(sgu_norm_g[l]), row(sgu_norm_b[l]), sgu_w_s[l], bs_exp,
            fox_heads=fox_heads, diff_heads=diff_heads)

        yft = _fox_attention(qa, ka, vft)
        ydt = _diff_attention(q2, k2, vdt, row(diff_lambda_q1[l]), row(diff_lambda_k1[l]),
                              row(diff_lambda_q2[l]), row(diff_lambda_k2[l]),
                              diff_norm_g[l].reshape(diff_heads, DIFF_V_DIM, 1), lam_init=lam_init)

        h = _out_ffn(yft, ydt, ys, h, w_out[l].astype(BF16), row(norm_m_g[l]), row(norm_m_b[l]),
                     ffn_b_w_gate[l].astype(BF16), ffn_b_w_up[l].astype(BF16), ffn_b_w_down[l].astype(BF16),
                     row(norm_b_g[l]), row(norm_b_b[l]), alpha=alpha)
    return h
```

```python
import functools
import math

import numpy as np
import jax
import jax.numpy as jnp
from jax import lax
from jax.experimental import pallas as pl
from jax.experimental.pallas import tpu as pltpu

F32 = jnp.float32
BF16 = jnp.bfloat16

HEAD_DIM = 64
DIFF_QK_DIM = 32
DIFF_V_DIM = 64
SGU_HEAD_DIM = 64
CHUNK = 128
LN_EPS = 1e-5
NEG_INF = -1e30
LOG2E = math.log2(math.e)
DV_AUG = HEAD_DIM + 16

LANES = 128
VMEM_LIMIT = 56 * 1024 * 1024

FOX_EXT = HEAD_DIM
DIFF_EXT = DIFF_QK_DIM

TQ = 512
TK = 256
MXU_DIM = 256
TM_FFN = 1024
FFN_SUB = 512
TM_IN = 512


def _nt_dot(a, b):
    return lax.dot_general(a, b, (((1,), (1,)), ((), ())), preferred_element_type=F32)


def _dot(a, b):
    return jnp.dot(a, b, preferred_element_type=F32)


def _layer_norm(y, g, b):
    mu = jnp.mean(y, axis=-1, keepdims=True)
    d = y - mu
    var = jnp.mean(d * d, axis=-1, keepdims=True)
    return d * lax.rsqrt(var + LN_EPS) * g + b


def _split3(x):
    hi = x.astype(BF16)
    r1 = x - hi.astype(F32)
    mid = r1.astype(BF16)
    r2 = r1 - mid.astype(F32)
    lo = r2.astype(BF16)
    return hi, mid, lo


def _split3_np(x):
    x = np.asarray(x, np.float32)
    hi = x.astype(BF16).astype(np.float32)
    r1 = x - hi
    mid = r1.astype(BF16).astype(np.float32)
    lo = (r1 - mid).astype(BF16).astype(np.float32)
    return hi, mid, lo


def _ffn_groups(load_rows, n_groups, sub, wg_ref, wu_ref, wd_ref, g_ref, b_ref, o_ref, *, alpha, chunks):
    x = load_rows(pl.ds(0, sub))
    finish = None
    for r in range(n_groups):
        xb = x.astype(BF16)
        acc = None
        x_next = None
        for ci, (c0, c1) in enumerate(chunks):
            gate = _dot(xb, wg_ref[:, c0:c1])
            up = _dot(xb, wu_ref[:, c0:c1])
            hidden = (gate * jax.nn.sigmoid(gate) * up).astype(BF16)
            part = _dot(hidden, wd_ref[c0:c1, :])
            acc = part if acc is None else acc + part
            if ci == 0:
                if finish is not None:
                    finish()
                if r + 1 < n_groups:
                    x_next = load_rows(pl.ds((r + 1) * sub, sub))
        y = alpha * x + 0.5 * acc

        def finish(y=y, rows=pl.ds(r * sub, sub)):
            o_ref[rows, :] = _layer_norm(y, g_ref[...], b_ref[...])
        x = x_next
    finish()


def _ffn_kernel(x_ref, wg_ref, wu_ref, wd_ref, g_ref, b_ref, o_ref, *, alpha, sub, chunks):
    _ffn_groups(lambda rows: x_ref[rows, :], x_ref.shape[0] // sub, sub,
                wg_ref, wu_ref, wd_ref, g_ref, b_ref, o_ref, alpha=alpha, chunks=chunks)


def _out_ffn_kernel(yf_ref, yd_ref, ys_ref, h_ref, wo_ref, gm_ref, bm_ref,
                    wg_ref, wu_ref, wd_ref, g_ref, b_ref, o_ref, *, alpha, sub, chunks):
    def mixed_rows(rows):
        yf = jnp.transpose(yf_ref[:, rows].astype(F32)).astype(BF16)
        yd = jnp.transpose(yd_ref[:, rows].astype(F32)).astype(BF16)
        y = jnp.concatenate([yf, yd, ys_ref[rows, :]], axis=1)
        return _layer_norm(alpha * h_ref[rows, :] + _dot(y, wo_ref[...]), gm_ref[...], bm_ref[...])

    _ffn_groups(mixed_rows, h_ref.shape[0] // sub, sub,
                wg_ref, wu_ref, wd_ref, g_ref, b_ref, o_ref, alpha=alpha, chunks=chunks)


def _resident(shape, layer=None):
    if layer is None:
        return pl.BlockSpec(shape, lambda *_: (0,) * len(shape), pipeline_mode=pl.Buffered(1))
    return pl.BlockSpec((None,) + tuple(shape[1:]), lambda *_: (layer,) + (0,) * (len(shape) - 1),
                        pipeline_mode=pl.Buffered(1))


def _hidden_chunks(f):
    half = (f // MXU_DIM + 1) // 2 * MXU_DIM
    return ((0, half), (half, f))


def _out_ffn(yft, ydt, ys, h, wo, gm, bm, wg, wu, wd, g, b, *, layer, alpha):
    bsz, s, d = h.shape
    fw, dw, sw = yft.shape[1], ydt.shape[1], ys.shape[2]
    tm = TM_FFN
    kern = functools.partial(_out_ffn_kernel, alpha=alpha, sub=FFN_SUB, chunks=_hidden_chunks(wg.shape[-1]))
    per_layer = lambda a: _resident(a.shape, layer)
    return pl.pallas_call(
        kern,
        grid=(bsz, s // tm),
        in_specs=[
            pl.BlockSpec((None, fw, tm), lambda bi, si: (bi, 0, si)),
            pl.BlockSpec((None, dw, tm), lambda bi, si: (bi, 0, si)),
            pl.BlockSpec((None, tm, sw), lambda bi, si: (bi, si, 0)),
            pl.BlockSpec((None, tm, d), lambda bi, si: (bi, si, 0)),
            per_layer(wo), per_layer(gm), per_layer(bm),
            per_layer(wg), per_layer(wu), per_layer(wd), per_layer(g), per_layer(b),
        ],
        out_specs=pl.BlockSpec((None, tm, d), lambda bi, si: (bi, si, 0)),
        out_shape=jax.ShapeDtypeStruct((bsz, s, d), F32),
        compiler_params=pltpu.CompilerParams(
            dimension_semantics=("parallel", "parallel"), vmem_limit_bytes=VMEM_LIMIT),
        name="out_ffn",
    )(yft, ydt, ys, h, wo, gm, bm, wg, wu, wd, g, b)


def _ffn_ln(x2d, wg, wu, wd, g, b, *, layer, alpha):
    t, d = x2d.shape
    tm = TM_FFN
    chunks = _hidden_chunks(wg.shape[-1])
    per_layer = lambda a: _resident(a.shape, layer)
    return pl.pallas_call(
        functools.partial(_ffn_kernel, alpha=alpha, sub=FFN_SUB, chunks=chunks),
        grid=(t // tm,),
        in_specs=[
            pl.BlockSpec((tm, d), lambda i: (i, 0)),
            per_layer(wg), per_layer(wu), per_layer(wd), per_layer(g), per_layer(b),
        ],
        out_specs=pl.BlockSpec((tm, d), lambda i: (i, 0)),
        out_shape=jax.ShapeDtypeStruct((t, d), F32),
        compiler_params=pltpu.CompilerParams(
            dimension_semantics=("parallel",), vmem_limit_bytes=VMEM_LIMIT),
        name="ffn_ln",
    )(x2d, wg, wu, wd, g, b)


def _mixer_in_kernel(h_ref, wqk_ref, wfvt_ref, wff_ref, fb_ref, wd_ref, wdvt_ref, tri_ref,
                     eqk_ref, ones_ref, ext2_ref, wsu_ref, wsg_ref, lg_ref, lb_ref, ws_ref, bs_ref,
                     qa_ref, ka_ref, vft_ref, q2_ref, k2_ref, vdt_ref, ys_ref, carry_ref,
                     *, fox_heads, diff_heads, diff_scale, fox_scale):
    s = pl.program_id(1)
    tm = h_ref.shape[0]
    hb = h_ref[...].astype(BF16)

    @pl.when(s == 0)
    def _():
        carry_ref[...] = jnp.zeros_like(carry_ref)

    fx = _dot(hb, wff_ref[...]) + fb_ref[...]
    nq = fox_heads * LANES
    pqk = _dot(hb, wqk_ref[...])

    log_f = -(jnp.maximum(-fx, 0.0) + jnp.log1p(jnp.exp(-jnp.abs(fx))))
    lane = lax.broadcasted_iota(jnp.int32, log_f.shape, 1)
    log_f = jnp.where(lane < fox_heads, log_f, 0.0)
    t3 = jnp.concatenate(_split3(log_f), axis=1)
    cs = _dot(tri_ref[...], t3)
    c = cs[:, :LANES] + cs[:, LANES:2 * LANES] + cs[:, 2 * LANES:] + carry_ref[...]
    carry_ref[...] = c[tm - 1:tm, :]
    c3 = jnp.concatenate(_split3(c * LOG2E), axis=1)
    ext = _dot(c3, eqk_ref[...])

    su = _dot(hb, wsu_ref[...])
    sg = _dot(hb, wsg_ref[...])

    for h in range(fox_heads):
        cq = slice(h * LANES, (h + 1) * LANES)
        ck = slice(nq + h * LANES, nq + (h + 1) * LANES)
        qa_ref[h] = (pqk[:, cq] * (fox_scale * LOG2E) + ext[:, :LANES] + ones_ref[:, cq]).astype(BF16)
        ka_ref[h] = (pqk[:, ck] + ext[:, LANES:] + ones_ref[:, ck]).astype(BF16)

    ones_rows = jnp.ones((DV_AUG - HEAD_DIM, TK), BF16)

    def store_values(vt, v_ref, heads):
        for cblk in range(tm // TK):
            for h in range(heads):
                v_ref[cblk, pl.ds(h * DV_AUG, HEAD_DIM), :] = (
                    vt[h * HEAD_DIM:(h + 1) * HEAD_DIM, cblk * TK:(cblk + 1) * TK])
                v_ref[cblk, pl.ds(h * DV_AUG + HEAD_DIM, DV_AUG - HEAD_DIM), :] = ones_rows

    store_values(_nt_dot(wfvt_ref[...], hb).astype(BF16), vft_ref, fox_heads)

    u = _gelu_tanh(su)
    g = _layer_norm(_gelu_tanh(sg), lg_ref[...], lb_ref[...]).astype(BF16)

    nd = diff_heads * LANES
    pd = _dot(hb, wd_ref[...])

    _sgu_mix(u, g, ws_ref, bs_ref, ys_ref)

    ext2 = ext2_ref[...].astype(F32)
    q2 = (pd[:, :nd] * (diff_scale * LOG2E) + ext2[:, :nd]).astype(BF16)
    k2 = (pd[:, nd:] + ext2[:, nd:]).astype(BF16)
    for h in range(diff_heads):
        q2_ref[h] = q2[:, h * LANES:(h + 1) * LANES]
        k2_ref[h] = k2[:, h * LANES:(h + 1) * LANES]

    store_values(_nt_dot(wdvt_ref[...], hb).astype(BF16), vdt_ref, diff_heads)


def _mixer_in(h, wqk, wfvt, wff, fb, wd, wdvt, tri, eqk, ones, ext2, wsu, wsg, lg, lb, ws, bs_exp,
              *, layer, fox_heads, diff_heads):
    b, s, d = h.shape
    tm = TM_IN
    nkv = s // TK
    fw = fox_heads * DV_AUG
    dw = diff_heads * DV_AUG
    const = _resident
    per_layer = lambda a: _resident(a.shape, layer)
    sgu_w = wsu.shape[-1]
    kern = functools.partial(
        _mixer_in_kernel, fox_heads=fox_heads, diff_heads=diff_heads,
        diff_scale=DIFF_QK_DIM ** -0.5, fox_scale=HEAD_DIM ** -0.5)
    return pl.pallas_call(
        kern,
        grid=(b, s // tm),
        in_specs=[
            pl.BlockSpec((None, tm, d), lambda bi, si: (bi, si, 0)),
            per_layer(wqk), per_layer(wfvt), per_layer(wff), per_layer(fb),
            per_layer(wd), per_layer(wdvt), const(tri.shape), const(eqk.shape),
            const(ones.shape),
            pl.BlockSpec((tm, ext2.shape[1]), lambda bi, si: (si, 0)),
            per_layer(wsu), per_layer(wsg), per_layer(lg), per_layer(lb),
            per_layer(ws), per_layer(bs_exp),
        ],
        out_specs=[
            pl.BlockSpec((None, fox_heads, tm, LANES), lambda bi, si: (bi, 0, si, 0)),
            pl.BlockSpec((None, fox_heads, tm, LANES), lambda bi, si: (bi, 0, si, 0)),
            pl.BlockSpec((None, tm // TK, fw, TK), lambda bi, si: (bi, si, 0, 0)),
            pl.BlockSpec((None, diff_heads, tm, LANES), lambda bi, si: (bi, 0, si, 0)),
            pl.BlockSpec((None, diff_heads, tm, LANES), lambda bi, si: (bi, 0, si, 0)),
            pl.BlockSpec((None, tm // TK, dw, TK), lambda bi, si: (bi, si, 0, 0)),
            pl.BlockSpec((None, tm, sgu_w), lambda bi, si: (bi, si, 0)),
        ],
        out_shape=[
            jax.ShapeDtypeStruct((b, fox_heads, s, LANES), BF16),
            jax.ShapeDtypeStruct((b, fox_heads, s, LANES), BF16),
            jax.ShapeDtypeStruct((b, nkv, fw, TK), BF16),
            jax.ShapeDtypeStruct((b, diff_heads, s, LANES), BF16),
            jax.ShapeDtypeStruct((b, diff_heads, s, LANES), BF16),
            jax.ShapeDtypeStruct((b, nkv, dw, TK), BF16),
            jax.ShapeDtypeStruct((b, s, sgu_w), BF16),
        ],
        scratch_shapes=[pltpu.VMEM((1, LANES), F32)],
        compiler_params=pltpu.CompilerParams(
            dimension_semantics=("parallel", "arbitrary"), vmem_limit_bytes=VMEM_LIMIT),
        name="mixer_in",
    )(h, wqk, wfvt, wff, fb, wd, wdvt, tri, eqk, ones, ext2, wsu, wsg, lg, lb, ws, bs_exp)


def _gelu_tanh(x):
    c = math.sqrt(2.0 / math.pi)
    return 0.5 * x * (1.0 + jnp.tanh(c * (x + 0.044715 * (x * x * x))))


def _sgu_mix(u, g, ws_ref, bs_ref, o_ref):
    tm = u.shape[0]
    heads = ws_ref.shape[0]
    row = lax.broadcasted_iota(jnp.int32, (CHUNK, CHUNK), 0)
    col = lax.broadcasted_iota(jnp.int32, (CHUNK, CHUNK), 1)
    causal = col <= row
    w = [jnp.where(causal, ws_ref[hh], 0.0).astype(BF16) for hh in range(heads)]
    lane = lax.broadcasted_iota(jnp.int32, (CHUNK, LANES), 1)
    low = lane < SGU_HEAD_DIM
    bias = bs_ref[...]
    for cblk in range(tm // CHUNK):
        rows = slice(cblk * CHUNK, (cblk + 1) * CHUNK)
        for p in range(heads // 2):
            cols = slice(p * LANES, (p + 1) * LANES)
            gp = g[rows, cols]
            zero = jnp.zeros_like(gp)
            mixed = (_dot(w[2 * p], jnp.where(low, gp, zero))
                     + _dot(w[2 * p + 1], jnp.where(low, zero, gp))
                     + bias[:, cols])
            o_ref[rows, cols] = (u[rows, cols] * mixed).astype(o_ref.dtype)


def _flash_sweep(qts, k_refs, v_refs, qi, scratch):
    n = len(qts)
    assert TQ == 2 * TK
    s_refs, bm_refs, m_ref, acc_ref, qt_ref = scratch[0:2], scratch[2:4], scratch[4], scratch[5], scratch[6]

    def scores(i, j, slot):
        start = pl.multiple_of(j * TK, TK)
        st = _dot(k_refs[i][pl.ds(start, TK), :], qt_ref[i])
        s_refs[slot][i] = st
        bm_refs[slot][i] = jnp.max(st, axis=0, keepdims=True)

    def step(i, j, slot):
        scores(i, j + 1, 1 - slot)
        m = m_ref[i]
        m_new = jnp.maximum(m, bm_refs[slot][i])
        p = jnp.exp2(s_refs[slot][i] - m_new).astype(BF16)
        m_ref[i] = m_new
        acc_ref[i] = acc_ref[i] * jnp.exp2(m - m_new) + _dot(v_refs[i][j], p)

    for i in range(n):
        qt_ref[i] = qts[i]()
        scores(i, 0, 0)
        m_ref[i] = jnp.full(m_ref.shape[1:], NEG_INF, F32)
        acc_ref[i] = jnp.zeros(acc_ref.shape[1:], F32)

    @pl.loop(0, qi)
    def _(u):
        for i in range(n):
            step(i, 2 * u, 0)
        for i in range(n):
            step(i, 2 * u + 1, 1)

    causal = (lax.broadcasted_iota(jnp.int32, (TK, TK), 0) <= lax.broadcasted_iota(jnp.int32, (TK, TK), 1))
    last = 2 * qi + 1
    for i in range(n):
        start = pl.multiple_of(last * TK, TK)
        s_refs[1][i, :, TK:] = _dot(k_refs[i][pl.ds(start, TK), :], qt_ref[i, :, TK:])
        st = s_refs[0][i]
        st = jnp.concatenate([jnp.where(causal, st[:, :TK], NEG_INF), st[:, TK:]], axis=1)
        m = m_ref[i]
        m_new = jnp.maximum(m, jnp.max(st, axis=0, keepdims=True))
        p = jnp.exp2(st - m_new).astype(BF16)
        m_ref[i] = m_new
        acc_ref[i] = acc_ref[i] * jnp.exp2(m - m_new) + _dot(v_refs[i][last - 1], p)
    for i in range(n):
        st = jnp.where(causal, s_refs[1][i, :, TK:], NEG_INF)
        m = m_ref[i, :, TK:]
        m_new = jnp.maximum(m, jnp.max(st, axis=0, keepdims=True))
        p = jnp.exp2(st - m_new).astype(BF16)
        acc_ref[i, :, TK:] = acc_ref[i, :, TK:] * jnp.exp2(m - m_new) + _dot(v_refs[i][last], p)


ACC_SCRATCH = 5


def _sweep_scratch(n):
    return ([pltpu.VMEM((n, TK, TQ), F32)] * 2 + [pltpu.VMEM((n, 1, TQ), F32)] * 2
            + [pltpu.VMEM((n, 1, TQ), F32), pltpu.VMEM((n, DV_AUG, TQ), F32),
               pltpu.VMEM((n, LANES, TQ), BF16)])


def _transposed(q):
    return jnp.transpose(q.astype(F32)).astype(BF16)


def _normalised(acc_ref, i):
    return acc_ref[i, :HEAD_DIM, :] / acc_ref[i, HEAD_DIM:HEAD_DIM + 1, :]


def _fox_kernel(q_ref, k_ref, v_ref, o_ref, *scratch, hp):
    qi = pl.program_id(2)
    qts = [functools.partial(lambda hh: _transposed(q_ref[hh]), hh) for hh in range(hp)]
    k_refs = [k_ref.at[hh] for hh in range(hp)]
    v_refs = [v_ref.at[:, pl.ds(hh * DV_AUG, DV_AUG), :] for hh in range(hp)]
    _flash_sweep(qts, k_refs, v_refs, qi, scratch)
    for hh in range(hp):
        o_ref[pl.ds(hh * HEAD_DIM, HEAD_DIM), :] = _normalised(scratch[ACC_SCRATCH], hh).astype(o_ref.dtype)


def _fox_attention(qa, ka, vft, *, hp=6):
    b, heads, s, _ = qa.shape
    nkv = vft.shape[1]
    return pl.pallas_call(
        functools.partial(_fox_kernel, hp=hp),
        grid=(b, heads // hp, s // TQ),
        in_specs=[
            pl.BlockSpec((None, hp, TQ, LANES), lambda bi, hi, qi: (bi, hi, qi, 0)),
            pl.BlockSpec((None, hp, s, LANES), lambda bi, hi, qi: (bi, hi, 0, 0)),
            pl.BlockSpec((None, nkv, hp * DV_AUG, TK), lambda bi, hi, qi: (bi, 0, hi, 0)),
        ],
        out_specs=pl.BlockSpec((None, hp * HEAD_DIM, TQ), lambda bi, hi, qi: (bi, hi, qi)),
        out_shape=jax.ShapeDtypeStruct((b, heads * HEAD_DIM, s), BF16),
        scratch_shapes=_sweep_scratch(hp),
        compiler_params=pltpu.CompilerParams(
            dimension_semantics=("parallel", "parallel", "arbitrary"),
            vmem_limit_bytes=VMEM_LIMIT),
        name="fox_attention",
    )(qa, ka, vft)


def _diff_kernel(q_ref, k_ref, v_ref, lq1_ref, lk1_ref, lq2_ref, lk2_ref, g_ref, o_ref,
                 *scratch, hp, lam_init):
    qi = pl.program_id(2)
    lane = lax.broadcasted_iota(jnp.int32, (TQ, LANES), 1)
    first_map = lane < 2 * DIFF_QK_DIM
    qts, k_refs, v_refs = [], [], []
    def one_map(hh, first):
        q2 = q_ref[hh]
        zero = jnp.zeros_like(q2)
        return _transposed(jnp.where(first_map, q2, zero) if first else jnp.where(first_map, zero, q2))

    for hh in range(hp):
        qts += [functools.partial(one_map, hh, True), functools.partial(one_map, hh, False)]
        k_refs += [k_ref.at[hh]] * 2
        v_refs += [v_ref.at[:, pl.ds(hh * DV_AUG, DV_AUG), :]] * 2
    _flash_sweep(qts, k_refs, v_refs, qi, scratch)

    lam = (jnp.exp(jnp.sum(lq1_ref[...] * lk1_ref[...], axis=-1, keepdims=True))
           - jnp.exp(jnp.sum(lq2_ref[...] * lk2_ref[...], axis=-1, keepdims=True))
           + lam_init)
    for hh in range(hp):
        o = _normalised(scratch[ACC_SCRATCH], 2 * hh) - lam * _normalised(scratch[ACC_SCRATCH], 2 * hh + 1)
        ms = jnp.mean(o * o, axis=0, keepdims=True)
        o = o * lax.rsqrt(ms + LN_EPS)
        o = o * g_ref[hh] * (1.0 - lam_init)
        o_ref[pl.ds(hh * DIFF_V_DIM, DIFF_V_DIM), :] = o.astype(o_ref.dtype)


def _diff_attention(q2, k2, vdt, lq1, lk1, lq2, lk2, gcol, *, layer, lam_init, hp=4):
    b, heads, s, _ = q2.shape
    nkv = vdt.shape[1]
    vec = lambda a: pl.BlockSpec((None,) + a.shape[1:], lambda bi, hi, qi: (layer, 0, 0))
    return pl.pallas_call(
        functools.partial(_diff_kernel, hp=hp, lam_init=lam_init),
        grid=(b, heads // hp, s // TQ),
        in_specs=[
            pl.BlockSpec((None, hp, TQ, LANES), lambda bi, hi, qi: (bi, hi, qi, 0)),
            pl.BlockSpec((None, hp, s, LANES), lambda bi, hi, qi: (bi, hi, 0, 0)),
            pl.BlockSpec((None, nkv, hp * DV_AUG, TK), lambda bi, hi, qi: (bi, 0, hi, 0)),
            vec(lq1), vec(lk1), vec(lq2), vec(lk2),
            pl.BlockSpec((None, hp, DIFF_V_DIM, 1), lambda bi, hi, qi: (layer, hi, 0, 0)),
        ],
        out_specs=pl.BlockSpec((None, hp * DIFF_V_DIM, TQ), lambda bi, hi, qi: (bi, hi, qi)),
        out_shape=jax.ShapeDtypeStruct((b, heads * DIFF_V_DIM, s), BF16),
        scratch_shapes=_sweep_scratch(2 * hp),
        compiler_params=pltpu.CompilerParams(
            dimension_semantics=("parallel", "parallel", "arbitrary"),
            vmem_limit_bytes=VMEM_LIMIT),
        name="diff_attention",
    )(q2, k2, vdt, lq1, lk1, lq2, lk2, gcol)


def _fox_bias_tables(fox_heads):
    nq = fox_heads * LANES
    nterm = 3 * fox_heads
    assert FOX_EXT + 2 * nterm <= LANES
    eqk = np.zeros((3 * LANES, 2 * LANES), np.float32)
    ones = np.zeros((1, 2 * nq), np.float32)
    for h in range(fox_heads):
        for t in range(3):
            a = FOX_EXT + fox_heads * t + h
            eqk[t * LANES + h, a + nterm] = 1.0
            eqk[t * LANES + h, LANES + a] = -1.0
            ones[0, h * LANES + a] = 1.0
            ones[0, nq + h * LANES + a + nterm] = 1.0
    return jnp.asarray(eqk, BF16), jnp.asarray(ones, F32)


def _alibi_tables(seq, diff_heads):
    nd = diff_heads * LANES
    ext = np.zeros((seq, 2 * nd), np.float32)
    pos = np.arange(seq, dtype=np.float32)
    slopes = np.float32(2.0) ** (np.float32(-8.0) * np.arange(1, diff_heads + 1, dtype=np.float32)
                                 / np.float32(diff_heads))
    for h in range(diff_heads):
        q_terms = _split3_np(np.float32(LOG2E) * (-slopes[h] * pos))
        k_terms = _split3_np(np.float32(LOG2E) * (slopes[h] * pos))
        for mp in range(2):
            base = h * LANES + mp * 2 * DIFF_QK_DIM + DIFF_EXT
            for t in range(3):
                ext[:, base + t] = 1.0
                ext[:, base + 3 + t] = q_terms[t]
                ext[:, nd + base + t] = k_terms[t]
                ext[:, nd + base + 3 + t] = 1.0
    return jnp.asarray(ext, BF16)


def _pad_heads(w, heads, dim):
    lead = w.shape[:-1]
    w = w.reshape(lead + (heads, dim))
    return jnp.pad(w, ((0, 0),) * (len(lead) + 1) + ((0, LANES - dim),)).reshape(lead + (heads * LANES,))


def _pad_maps(w, heads):
    lead = w.shape[:-1]
    w = w.reshape(lead + (heads, 2, DIFF_QK_DIM))
    w = jnp.pad(w, ((0, 0),) * (len(lead) + 2) + ((0, DIFF_QK_DIM),))
    return w.reshape(lead + (heads * LANES,))


def kernel(x, ffn_a_w_gate, ffn_a_w_up, ffn_a_w_down, norm_a_g, norm_a_b, w_in, fox_f_bias, diff_lambda_q1, diff_lambda_k1, diff_lambda_q2, diff_lambda_k2, diff_norm_g, sgu_norm_g, sgu_norm_b, sgu_w_s, sgu_b_s, w_out, norm_m_g, norm_m_b, ffn_b_w_gate, ffn_b_w_up, ffn_b_w_down, norm_b_g, norm_b_b):
    bsz, seq, d = x.shape
    depth = w_in.shape[0]
    fox_heads = fox_f_bias.shape[1]
    fox_w = fox_heads * HEAD_DIM
    diff_w = diff_norm_g.shape[1]
    diff_heads = diff_w // DIFF_V_DIM
    sgu_w = sgu_norm_g.shape[1]
    alpha = (2 * depth) ** 0.25

    eqk, ones = _fox_bias_tables(fox_heads)
    ext2 = _alibi_tables(seq, diff_heads)
    tri = jnp.asarray(np.tril(np.ones((TM_IN, TM_IN), np.float32)), BF16)

    bf = lambda w: w.astype(BF16)
    rows = lambda v: v.reshape(depth, 1, -1)
    wa = (bf(ffn_a_w_gate), bf(ffn_a_w_up), bf(ffn_a_w_down), rows(norm_a_g), rows(norm_a_b))
    wb = (bf(ffn_b_w_gate), bf(ffn_b_w_up), bf(ffn_b_w_down), rows(norm_b_g), rows(norm_b_b))
    wo = (bf(w_out), rows(norm_m_g), rows(norm_m_b))

    wi = bf(w_in)
    i1 = 3 * fox_w
    i2 = i1 + fox_heads
    i3 = i2 + 3 * diff_w
    wqk = jnp.concatenate([_pad_heads(wi[..., :fox_w], fox_heads, HEAD_DIM),
                           _pad_heads(wi[..., fox_w:2 * fox_w], fox_heads, HEAD_DIM)], axis=-1)
    wfvt = jnp.swapaxes(wi[..., 2 * fox_w:i1], 1, 2)
    wff = jnp.pad(wi[..., i1:i2], ((0, 0), (0, 0), (0, LANES - fox_heads)))
    fb = rows(jnp.pad(fox_f_bias, ((0, 0), (0, LANES - fox_heads))))
    wdqk = jnp.concatenate([_pad_maps(wi[..., i2:i2 + diff_w], diff_heads),
                            _pad_maps(wi[..., i2 + diff_w:i2 + 2 * diff_w], diff_heads)], axis=-1)
    wdvt = jnp.swapaxes(wi[..., i2 + 2 * diff_w:i3], 1, 2)
    wsu, wsg = wi[..., i3:i3 + sgu_w], wi[..., i3 + sgu_w:]
    bs_exp = jnp.repeat(jnp.swapaxes(sgu_b_s, 1, 2), SGU_HEAD_DIM, axis=2)
    lam = tuple(rows(v) for v in (diff_lambda_q1, diff_lambda_k1, diff_lambda_q2, diff_lambda_k2))
    gcol = diff_norm_g.reshape(depth, diff_heads, DIFF_V_DIM, 1)

    h = x
    for l in range(depth):
        lam_init = 0.8 - 0.6 * math.exp(-0.3 * l)
        h = _ffn_ln(h.reshape(bsz * seq, d), *wa, layer=l, alpha=alpha).reshape(bsz, seq, d)
        qa, ka, vft, q2, k2, vdt, ys = _mixer_in(
            h, wqk, wfvt, wff, fb, wdqk, wdvt, tri, eqk, ones, ext2,
            wsu, wsg, rows(sgu_norm_g), rows(sgu_norm_b), sgu_w_s, bs_exp,
            layer=l, fox_heads=fox_heads, diff_heads=diff_heads)
        yft = _fox_attention(qa, ka, vft)
        ydt = _diff_attention(q2, k2, vdt, *lam, gcol, layer=l, lam_init=lam_init)
        h = _out_ffn(yft, ydt, ys, h, *wo, *wb, layer=l, alpha=alpha)
    return h
```

```python
import functools
import math

import numpy as np
import jax
import jax.numpy as jnp
from jax import lax
from jax.experimental import pallas as pl
from jax.experimental.pallas import tpu as pltpu

F32 = jnp.float32
BF16 = jnp.bfloat16

HEAD_DIM = 64
DIFF_QK_DIM = 32
DIFF_V_DIM = 64
SGU_HEAD_DIM = 64
CHUNK = 128
LN_EPS = 1e-5
NEG_INF = -1e30
LOG2E = math.log2(math.e)
DV_AUG = HEAD_DIM + 16

LANES = 128
VMEM_LIMIT = 56 * 1024 * 1024

FOX_EXT = HEAD_DIM
DIFF_EXT = DIFF_QK_DIM

TQ = 512
TK = 256
MXU_DIM = 256
TM_FFN = 1024
FFN_SUB = 512
TM_IN = 512


def _nt_dot(a, b):
    return lax.dot_general(a, b, (((1,), (1,)), ((), ())), preferred_element_type=F32)


def _dot(a, b):
    return jnp.dot(a, b, preferred_element_type=F32)


def _layer_norm(y, g, b):
    mu = jnp.mean(y, axis=-1, keepdims=True)
    d = y - mu
    var = jnp.mean(d * d, axis=-1, keepdims=True)
    return d * lax.rsqrt(var + LN_EPS) * g + b


def _split3(x):
    hi = x.astype(BF16)
    r1 = x - hi.astype(F32)
    mid = r1.astype(BF16)
    r2 = r1 - mid.astype(F32)
    lo = r2.astype(BF16)
    return hi, mid, lo


def _split3_np(x):
    x = np.asarray(x, np.float32)
    hi = x.astype(BF16).astype(np.float32)
    r1 = x - hi
    mid = r1.astype(BF16).astype(np.float32)
    lo = (r1 - mid).astype(BF16).astype(np.float32)
    return hi, mid, lo


def _ffn_groups(load_rows, n_groups, sub, wg_ref, wu_ref, wd_ref, g_ref, b_ref, o_ref, *, alpha, chunks):
    x = load_rows(pl.ds(0, sub))
    finish = None
    for r in range(n_groups):
        xb = x.astype(BF16)
        acc = None
        x_next = None
        for ci, (c0, c1) in enumerate(chunks):
            gate = _dot(xb, wg_ref[:, c0:c1])
            up = _dot(xb, wu_ref[:, c0:c1])
            hidden = (gate * jax.nn.sigmoid(gate) * up).astype(BF16)
            part = _dot(hidden, wd_ref[c0:c1, :])
            acc = part if acc is None else acc + part
            if ci == 0:
                if finish is not None:
                    finish()
                if r + 1 < n_groups:
                    x_next = load_rows(pl.ds((r + 1) * sub, sub))
        y = alpha * x + 0.5 * acc

        def finish(y=y, rows=pl.ds(r * sub, sub)):
            o_ref[rows, :] = _layer_norm(y, g_ref[...], b_ref[...])
        x = x_next
    finish()


def _ffn_kernel(x_ref, wg_ref, wu_ref, wd_ref, g_ref, b_ref, o_ref, *, alpha, sub, chunks):
    _ffn_groups(lambda rows: x_ref[rows, :], x_ref.shape[0] // sub, sub,
                wg_ref, wu_ref, wd_ref, g_ref, b_ref, o_ref, alpha=alpha, chunks=chunks)


def _out_ffn_kernel(yf_ref, yd_ref, ys_ref, h_ref, wo_ref, gm_ref, bm_ref,
                    wg_ref, wu_ref, wd_ref, g_ref, b_ref, o_ref, *, alpha, sub, chunks):
    def mixed_rows(rows):
        yf = jnp.transpose(yf_ref[:, rows].astype(F32)).astype(BF16)
        yd = jnp.transpose(yd_ref[:, rows].astype(F32)).astype(BF16)
        y = jnp.concatenate([yf, yd, ys_ref[rows, :]], axis=1)
        return _layer_norm(alpha * h_ref[rows, :] + _dot(y, wo_ref[...]), gm_ref[...], bm_ref[...])

    _ffn_groups(mixed_rows, h_ref.shape[0] // sub, sub,
                wg_ref, wu_ref, wd_ref, g_ref, b_ref, o_ref, alpha=alpha, chunks=chunks)


def _resident(shape, layer=None):
    if layer is None:
        return pl.BlockSpec(shape, lambda *_: (0,) * len(shape), pipeline_mode=pl.Buffered(1))
    return pl.BlockSpec((None,) + tuple(shape[1:]), lambda *_: (layer,) + (0,) * (len(shape) - 1),
                        pipeline_mode=pl.Buffered(1))


def _hidden_chunks(f):
    half = (f // MXU_DIM + 1) // 2 * MXU_DIM
    return ((0, half), (half, f))


def _out_ffn(yft, ydt, ys, h, wo, gm, bm, wg, wu, wd, g, b, *, layer, alpha):
    bsz, s, d = h.shape
    fw, dw, sw = yft.shape[1], ydt.shape[1], ys.shape[2]
    tm = TM_FFN
    kern = functools.partial(_out_ffn_kernel, alpha=alpha, sub=FFN_SUB, chunks=_hidden_chunks(wg.shape[-1]))
    per_layer = lambda a: _resident(a.shape, layer)
    return pl.pallas_call(
        kern,
        grid=(bsz, s // tm),
        in_specs=[
            pl.BlockSpec((None, fw, tm), lambda bi, si: (bi, 0, si)),
            pl.BlockSpec((None, dw, tm), lambda bi, si: (bi, 0, si)),
            pl.BlockSpec((None, tm, sw), lambda bi, si: (bi, si, 0)),
            pl.BlockSpec((None, tm, d), lambda bi, si: (bi, si, 0)),
            per_layer(wo), per_layer(gm), per_layer(bm),
            per_layer(wg), per_layer(wu), per_layer(wd), per_layer(g), per_layer(b),
        ],
        out_specs=pl.BlockSpec((None, tm, d), lambda bi, si: (bi, si, 0)),
        out_shape=jax.ShapeDtypeStruct((bsz, s, d), F32),
        compiler_params=pltpu.CompilerParams(
            dimension_semantics=("parallel", "parallel"), vmem_limit_bytes=VMEM_LIMIT),
        name="out_ffn",
    )(yft, ydt, ys, h, wo, gm, bm, wg, wu, wd, g, b)


def _ffn_ln(x2d, wg, wu, wd, g, b, *, layer, alpha):
    t, d = x2d.shape
    tm = TM_FFN
    chunks = _hidden_chunks(wg.shape[-1])
    per_layer = lambda a: _resident(a.shape, layer)
    return pl.pallas_call(
        functools.partial(_ffn_kernel, alpha=alpha, sub=FFN_SUB, chunks=chunks),
        grid=(t // tm,),
        in_specs=[
            pl.BlockSpec((tm, d), lambda i: (i, 0)),
            per_layer(wg), per_layer(wu), per_layer(wd), per_layer(g), per_layer(b),
        ],
        out_specs=pl.BlockSpec((tm, d), lambda i: (i, 0)),
        out_shape=jax.ShapeDtypeStruct((t, d), F32),
        compiler_params=pltpu.CompilerParams(
            dimension_semantics=("parallel",), vmem_limit_bytes=VMEM_LIMIT),
        name="ffn_ln",
    )(x2d, wg, wu, wd, g, b)


def _mixer_in_kernel(h_ref, wqk_ref, wfvt_ref, wff_ref, fb_ref, wd_ref, wdvt_ref, tri_ref,
                     eqk_ref, ones_ref, ext2_ref, wsu_ref, wsg_ref, lg_ref, lb_ref, ws_ref, bs_ref,
                     qa_ref, ka_ref, vft_ref, q2_ref, k2_ref, vdt_ref, ys_ref, carry_ref,
                     *, fox_heads, diff_heads, diff_scale, fox_scale):
    s = pl.program_id(1)
    tm = h_ref.shape[0]
    hb = h_ref[...].astype(BF16)

    @pl.when(s == 0)
    def _():
        carry_ref[...] = jnp.zeros_like(carry_ref)

    fx = _dot(hb, wff_ref[...]) + fb_ref[...]
    nq = fox_heads * LANES
    pqk = _dot(hb, wqk_ref[...])

    log_f = -(jnp.maximum(-fx, 0.0) + jnp.log1p(jnp.exp(-jnp.abs(fx))))
    lane = lax.broadcasted_iota(jnp.int32, log_f.shape, 1)
    log_f = jnp.where(lane < fox_heads, log_f, 0.0)
    t3 = jnp.concatenate(_split3(log_f), axis=1)
    cs = _dot(tri_ref[...], t3)
    c = cs[:, :LANES] + cs[:, LANES:2 * LANES] + cs[:, 2 * LANES:] + carry_ref[...]
    carry_ref[...] = c[tm - 1:tm, :]
    c3 = jnp.concatenate(_split3(c * LOG2E), axis=1)
    ext = _dot(c3, eqk_ref[...])

    su = _dot(hb, wsu_ref[...])
    sg = _dot(hb, wsg_ref[...])

    for h in range(fox_heads):
        cq = slice(h * LANES, (h + 1) * LANES)
        ck = slice(nq + h * LANES, nq + (h + 1) * LANES)
        qa_ref[h] = (pqk[:, cq] * (fox_scale * LOG2E) + ext[:, :LANES] + ones_ref[:, cq]).astype(BF16)
        ka_ref[h] = (pqk[:, ck] + ext[:, LANES:] + ones_ref[:, ck]).astype(BF16)

    ones_rows = jnp.ones((DV_AUG - HEAD_DIM, TK), BF16)

    def store_values(vt, v_ref, heads):
        for cblk in range(tm // TK):
            for h in range(heads):
                v_ref[cblk, pl.ds(h * DV_AUG, HEAD_DIM), :] = (
                    vt[h * HEAD_DIM:(h + 1) * HEAD_DIM, cblk * TK:(cblk + 1) * TK])
                v_ref[cblk, pl.ds(h * DV_AUG + HEAD_DIM, DV_AUG - HEAD_DIM), :] = ones_rows

    store_values(_nt_dot(wfvt_ref[...], hb).astype(BF16), vft_ref, fox_heads)

    u = _gelu_tanh(su)
    g = _layer_norm(_gelu_tanh(sg), lg_ref[...], lb_ref[...]).astype(BF16)

    nd = diff_heads * LANES
    pd = _dot(hb, wd_ref[...])

    _sgu_mix(u, g, ws_ref, bs_ref, ys_ref)

    ext2 = ext2_ref[...].astype(F32)
    q2 = (pd[:, :nd] * (diff_scale * LOG2E) + ext2[:, :nd]).astype(BF16)
    k2 = (pd[:, nd:] + ext2[:, nd:]).astype(BF16)
    for h in range(diff_heads):
        q2_ref[h] = q2[:, h * LANES:(h + 1) * LANES]
        k2_ref[h] = k2[:, h * LANES:(h + 1) * LANES]

    store_values(_nt_dot(wdvt_ref[...], hb).astype(BF16), vdt_ref, diff_heads)


def _mixer_in(h, wqk, wfvt, wff, fb, wd, wdvt, tri, eqk, ones, ext2, wsu, wsg, lg, lb, ws, bs_exp,
              *, layer, fox_heads, diff_heads):
    b, s, d = h.shape
    tm = TM_IN
    nkv = s // TK
    fw = fox_heads * DV_AUG
    dw = diff_heads * DV_AUG
    const = _resident
    per_layer = lambda a: _resident(a.shape, layer)
    sgu_w = wsu.shape[-1]
    kern = functools.partial(
        _mixer_in_kernel, fox_heads=fox_heads, diff_heads=diff_heads,
        diff_scale=DIFF_QK_DIM ** -0.5, fox_scale=HEAD_DIM ** -0.5)
    return pl.pallas_call(
        kern,
        grid=(b, s // tm),
        in_specs=[
            pl.BlockSpec((None, tm, d), lambda bi, si: (bi, si, 0)),
            per_layer(wqk), per_layer(wfvt), per_layer(wff), per_layer(fb),
            per_layer(wd), per_layer(wdvt), const(tri.shape), const(eqk.shape),
            const(ones.shape),
            pl.BlockSpec((tm, ext2.shape[1]), lambda bi, si: (si, 0)),
            per_layer(wsu), per_layer(wsg), per_layer(lg), per_layer(lb),
            per_layer(ws), per_layer(bs_exp),
        ],
        out_specs=[
            pl.BlockSpec((None, fox_heads, tm, LANES), lambda bi, si: (bi, 0, si, 0)),
            pl.BlockSpec((None, fox_heads, tm, LANES), lambda bi, si: (bi, 0, si, 0)),
            pl.BlockSpec((None, tm // TK, fw, TK), lambda bi, si: (bi, si, 0, 0)),
            pl.BlockSpec((None, diff_heads, tm, LANES), lambda bi, si: (bi, 0, si, 0)),
            pl.BlockSpec((None, diff_heads, tm, LANES), lambda bi, si: (bi, 0, si, 0)),
            pl.BlockSpec((None, tm // TK, dw, TK), lambda bi, si: (bi, si, 0, 0)),
            pl.BlockSpec((None, tm, sgu_w), lambda bi, si: (bi, si, 0)),
        ],
        out_shape=[
            jax.ShapeDtypeStruct((b, fox_heads, s, LANES), BF16),
            jax.ShapeDtypeStruct((b, fox_heads, s, LANES), BF16),
            jax.ShapeDtypeStruct((b, nkv, fw, TK), BF16),
            jax.ShapeDtypeStruct((b, diff_heads, s, LANES), BF16),
            jax.ShapeDtypeStruct((b, diff_heads, s, LANES), BF16),
            jax.ShapeDtypeStruct((b, nkv, dw, TK), BF16),
            jax.ShapeDtypeStruct((b, s, sgu_w), BF16),
        ],
        scratch_shapes=[pltpu.VMEM((1, LANES), F32)],
        compiler_params=pltpu.CompilerParams(
            dimension_semantics=("parallel", "arbitrary"), vmem_limit_bytes=VMEM_LIMIT),
        name="mixer_in",
    )(h, wqk, wfvt, wff, fb, wd, wdvt, tri, eqk, ones, ext2, wsu, wsg, lg, lb, ws, bs_exp)


def _gelu_tanh(x):
    c = math.sqrt(2.0 / math.pi)
    return 0.5 * x * (1.0 + jnp.tanh(c * (x + 0.044715 * (x * x * x))))


def _sgu_mix(u, g, ws_ref, bs_ref, o_ref):
    tm = u.shape[0]
    heads = ws_ref.shape[0]
    row = lax.broadcasted_iota(jnp.int32, (CHUNK, CHUNK), 0)
    col = lax.broadcasted_iota(jnp.int32, (CHUNK, CHUNK), 1)
    causal = col <= row
    w = [jnp.where(causal, ws_ref[hh], 0.0).astype(BF16) for hh in range(heads)]
    lane = lax.broadcasted_iota(jnp.int32, (CHUNK, LANES), 1)
    low = lane < SGU_HEAD_DIM
    bias = bs_ref[...]
    for cblk in range(tm // CHUNK):
        rows = slice(cblk * CHUNK, (cblk + 1) * CHUNK)
        for p in range(heads // 2):
            cols = slice(p * LANES, (p + 1) * LANES)
            gp = g[rows, cols]
            zero = jnp.zeros_like(gp)
            mixed = (_dot(w[2 * p], jnp.where(low, gp, zero))
                     + _dot(w[2 * p + 1], jnp.where(low, zero, gp))
                     + bias[:, cols])
            o_ref[rows, cols] = (u[rows, cols] * mixed).astype(o_ref.dtype)


def _flash_sweep(qts, k_refs, v_refs, qi, scratch):
    n = len(qts)
    assert TQ == 2 * TK
    s_refs, bm_refs, m_ref, acc_ref, qt_ref = scratch[0:2], scratch[2:4], scratch[4], scratch[5], scratch[6]

    def scores(i, j, slot):
        start = pl.multiple_of(j * TK, TK)
        st = _dot(k_refs[i][pl.ds(start, TK), :], qt_ref[i])
        s_refs[slot][i] = st
        bm_refs[slot][i] = jnp.max(st, axis=0, keepdims=True)

    def step(i, j, slot):
        scores(i, j + 1, 1 - slot)
        m = m_ref[i]
        m_new = jnp.maximum(m, bm_refs[slot][i])
        p = jnp.exp2(s_refs[slot][i] - m_new).astype(BF16)
        m_ref[i] = m_new
        acc_ref[i] = acc_ref[i] * jnp.exp2(m - m_new) + _dot(v_refs[i][j], p)

    for i in range(n):
        qt_ref[i] = qts[i]()
        scores(i, 0, 0)
        m_ref[i] = jnp.full(m_ref.shape[1:], NEG_INF, F32)
        acc_ref[i] = jnp.zeros(acc_ref.shape[1:], F32)

    def pair(first):
        for i in range(n):
            step(i, first, 0)
        for i in range(n):
            step(i, first + 1, 1)

    @pl.loop(0, lax.shift_right_logical(qi, 1))
    def _(u):
        pair(4 * u)
        pair(4 * u + 2)

    @pl.when(jnp.bitwise_and(qi, 1) == 1)
    def _():
        pair(2 * qi - 2)

    causal = (lax.broadcasted_iota(jnp.int32, (TK, TK), 0) <= lax.broadcasted_iota(jnp.int32, (TK, TK), 1))
    last = 2 * qi + 1

    def first_diagonal(i):
        start = pl.multiple_of(last * TK, TK)
        s_refs[1][i, :, TK:] = _dot(k_refs[i][pl.ds(start, TK), :], qt_ref[i, :, TK:])
        st = s_refs[0][i]
        st = jnp.concatenate([jnp.where(causal, st[:, :TK], NEG_INF), st[:, TK:]], axis=1)
        m = m_ref[i]
        m_new = jnp.maximum(m, jnp.max(st, axis=0, keepdims=True))
        p = jnp.exp2(st - m_new).astype(BF16)
        m_ref[i] = m_new
        acc_ref[i] = acc_ref[i] * jnp.exp2(m - m_new) + _dot(v_refs[i][last - 1], p)

    def second_diagonal(i):
        st = jnp.where(causal, s_refs[1][i, :, TK:], NEG_INF)
        m = m_ref[i, :, TK:]
        m_new = jnp.maximum(m, jnp.max(st, axis=0, keepdims=True))
        p = jnp.exp2(st - m_new).astype(BF16)
        acc_ref[i, :, TK:] = acc_ref[i, :, TK:] * jnp.exp2(m - m_new) + _dot(v_refs[i][last], p)

    for i in range(n):
        first_diagonal(i)
    for i in range(n):
        second_diagonal(i)


ACC_SCRATCH = 5


def _sweep_scratch(n):
    return ([pltpu.VMEM((n, TK, TQ), F32)] * 2 + [pltpu.VMEM((n, 1, TQ), F32)] * 2
            + [pltpu.VMEM((n, 1, TQ), F32), pltpu.VMEM((n, DV_AUG, TQ), F32),
               pltpu.VMEM((n, LANES, TQ), BF16)])


def _transposed(q):
    return jnp.transpose(q.astype(F32)).astype(BF16)


def _normalised(acc_ref, i):
    return acc_ref[i, :HEAD_DIM, :] / acc_ref[i, HEAD_DIM:HEAD_DIM + 1, :]


def _fox_kernel(q_ref, k_ref, v_ref, o_ref, *scratch, hp):
    qi = pl.program_id(2)
    qts = [functools.partial(lambda hh: _transposed(q_ref[hh]), hh) for hh in range(hp)]
    k_refs = [k_ref.at[hh] for hh in range(hp)]
    v_refs = [v_ref.at[:, pl.ds(hh * DV_AUG, DV_AUG), :] for hh in range(hp)]
    _flash_sweep(qts, k_refs, v_refs, qi, scratch)
    for hh in range(hp):
        o_ref[pl.ds(hh * HEAD_DIM, HEAD_DIM), :] = _normalised(scratch[ACC_SCRATCH], hh).astype(o_ref.dtype)


def _fox_attention(qa, ka, vft, *, hp=6):
    b, heads, s, _ = qa.shape
    nkv = vft.shape[1]
    return pl.pallas_call(
        functools.partial(_fox_kernel, hp=hp),
        grid=(b, heads // hp, s // TQ),
        in_specs=[
            pl.BlockSpec((None, hp, TQ, LANES), lambda bi, hi, qi: (bi, hi, qi, 0)),
            pl.BlockSpec((None, hp, s, LANES), lambda bi, hi, qi: (bi, hi, 0, 0)),
            pl.BlockSpec((None, nkv, hp * DV_AUG, TK), lambda bi, hi, qi: (bi, 0, hi, 0)),
        ],
        out_specs=pl.BlockSpec((None, hp * HEAD_DIM, TQ), lambda bi, hi, qi: (bi, hi, qi)),
        out_shape=jax.ShapeDtypeStruct((b, heads * HEAD_DIM, s), BF16),
        scratch_shapes=_sweep_scratch(hp),
        compiler_params=pltpu.CompilerParams(
            dimension_semantics=("parallel", "parallel", "arbitrary"),
            vmem_limit_bytes=VMEM_LIMIT),
        name="fox_attention",
    )(qa, ka, vft)


def _diff_kernel(q_ref, k_ref, v_ref, lq1_ref, lk1_ref, lq2_ref, lk2_ref, g_ref, o_ref,
                 *scratch, hp, lam_init):
    qi = pl.program_id(2)
    lane = lax.broadcasted_iota(jnp.int32, (TQ, LANES), 1)
    first_map = lane < 2 * DIFF_QK_DIM
    qts, k_refs, v_refs = [], [], []
    def one_map(hh, first):
        q2 = q_ref[hh]
        zero = jnp.zeros_like(q2)
        return _transposed(jnp.where(first_map, q2, zero) if first else jnp.where(first_map, zero, q2))

    for hh in range(hp):
        qts += [functools.partial(one_map, hh, True), functools.partial(one_map, hh, False)]
        k_refs += [k_ref.at[hh]] * 2
        v_refs += [v_ref.at[:, pl.ds(hh * DV_AUG, DV_AUG), :]] * 2
    _flash_sweep(qts, k_refs, v_refs, qi, scratch)

    lam = (jnp.exp(jnp.sum(lq1_ref[...] * lk1_ref[...], axis=-1, keepdims=True))
           - jnp.exp(jnp.sum(lq2_ref[...] * lk2_ref[...], axis=-1, keepdims=True))
           + lam_init)
    for hh in range(hp):
        o = _normalised(scratch[ACC_SCRATCH], 2 * hh) - lam * _normalised(scratch[ACC_SCRATCH], 2 * hh + 1)
        ms = jnp.mean(o * o, axis=0, keepdims=True)
        o = o * lax.rsqrt(ms + LN_EPS)
        o = o * g_ref[hh] * (1.0 - lam_init)
        o_ref[pl.ds(hh * DIFF_V_DIM, DIFF_V_DIM), :] = o.astype(o_ref.dtype)


def _diff_attention(q2, k2, vdt, lq1, lk1, lq2, lk2, gcol, *, layer, lam_init, hp=4):
    b, heads, s, _ = q2.shape
    nkv = vdt.shape[1]
    vec = lambda a: pl.BlockSpec((None,) + a.shape[1:], lambda bi, hi, qi: (layer, 0, 0))
    return pl.pallas_call(
        functools.partial(_diff_kernel, hp=hp, lam_init=lam_init),
        grid=(b, heads // hp, s // TQ),
        in_specs=[
            pl.BlockSpec((None, hp, TQ, LANES), lambda bi, hi, qi: (bi, hi, qi, 0)),
            pl.BlockSpec((None, hp, s, LANES), lambda bi, hi, qi: (bi, hi, 0, 0)),
            pl.BlockSpec((None, nkv, hp * DV_AUG, TK), lambda bi, hi, qi: (bi, 0, hi, 0)),
            vec(lq1), vec(lk1), vec(lq2), vec(lk2),
            pl.BlockSpec((None, hp, DIFF_V_DIM, 1), lambda bi, hi, qi: (layer, hi, 0, 0)),
        ],
        out_specs=pl.BlockSpec((None, hp * DIFF_V_DIM, TQ), lambda bi, hi, qi: (bi, hi, qi)),
        out_shape=jax.ShapeDtypeStruct((b, heads * DIFF_V_DIM, s), BF16),
        scratch_shapes=_sweep_scratch(2 * hp),
        compiler_params=pltpu.CompilerParams(
            dimension_semantics=("parallel", "parallel", "arbitrary"),
            vmem_limit_bytes=VMEM_LIMIT),
        name="diff_attention",
    )(q2, k2, vdt, lq1, lk1, lq2, lk2, gcol)


def _fox_bias_tables(fox_heads):
    nq = fox_heads * LANES
    nterm = 3 * fox_heads
    assert FOX_EXT + 2 * nterm <= LANES
    eqk = np.zeros((3 * LANES, 2 * LANES), np.float32)
    ones = np.zeros((1, 2 * nq), np.float32)
    for h in range(fox_heads):
        for t in range(3):
            a = FOX_EXT + fox_heads * t + h
            eqk[t * LANES + h, a + nterm] = 1.0
            eqk[t * LANES + h, LANES + a] = -1.0
            ones[0, h * LANES + a] = 1.0
            ones[0, nq + h * LANES + a + nterm] = 1.0
    return jnp.asarray(eqk, BF16), jnp.asarray(ones, F32)


def _alibi_tables(seq, diff_heads):
    nd = diff_heads * LANES
    ext = np.zeros((seq, 2 * nd), np.float32)
    pos = np.arange(seq, dtype=np.float32)
    slopes = np.float32(2.0) ** (np.float32(-8.0) * np.arange(1, diff_heads + 1, dtype=np.float32)
                                 / np.float32(diff_heads))
    for h in range(diff_heads):
        q_terms = _split3_np(np.float32(LOG2E) * (-slopes[h] * pos))
        k_terms = _split3_np(np.float32(LOG2E) * (slopes[h] * pos))
        for mp in range(2):
            base = h * LANES + mp * 2 * DIFF_QK_DIM + DIFF_EXT
            for t in range(3):
                ext[:, base + t] = 1.0
                ext[:, base + 3 + t] = q_terms[t]
                ext[:, nd + base + t] = k_terms[t]
                ext[:, nd + base + 3 + t] = 1.0
    return jnp.asarray(ext, BF16)


def _pad_heads(w, heads, dim):
    lead = w.shape[:-1]
    w = w.reshape(lead + (heads, dim))
    return jnp.pad(w, ((0, 0),) * (len(lead) + 1) + ((0, LANES - dim),)).reshape(lead + (heads * LANES,))


def _pad_maps(w, heads):
    lead = w.shape[:-1]
    w = w.reshape(lead + (heads, 2, DIFF_QK_DIM))
    w = jnp.pad(w, ((0, 0),) * (len(lead) + 2) + ((0, DIFF_QK_DIM),))
    return w.reshape(lead + (heads * LANES,))


def kernel(x, ffn_a_w_gate, ffn_a_w_up, ffn_a_w_down, norm_a_g, norm_a_b, w_in, fox_f_bias, diff_lambda_q1, diff_lambda_k1, diff_lambda_q2, diff_lambda_k2, diff_norm_g, sgu_norm_g, sgu_norm_b, sgu_w_s, sgu_b_s, w_out, norm_m_g, norm_m_b, ffn_b_w_gate, ffn_b_w_up, ffn_b_w_down, norm_b_g, norm_b_b):
    bsz, seq, d = x.shape
    depth = w_in.shape[0]
    fox_heads = fox_f_bias.shape[1]
    fox_w = fox_heads * HEAD_DIM
    diff_w = diff_norm_g.shape[1]
    diff_heads = diff_w // DIFF_V_DIM
    sgu_w = sgu_norm_g.shape[1]
    alpha = (2 * depth) ** 0.25

    eqk, ones = _fox_bias_tables(fox_heads)
    ext2 = _alibi_tables(seq, diff_heads)
    tri = jnp.asarray(np.tril(np.ones((TM_IN, TM_IN), np.float32)), BF16)

    bf = lambda w: w.astype(BF16)
    rows = lambda v: v.reshape(depth, 1, -1)
    wa = (bf(ffn_a_w_gate), bf(ffn_a_w_up), bf(ffn_a_w_down), rows(norm_a_g), rows(norm_a_b))
    wb = (bf(ffn_b_w_gate), bf(ffn_b_w_up), bf(ffn_b_w_down), rows(norm_b_g), rows(norm_b_b))
    wo = (bf(w_out), rows(norm_m_g), rows(norm_m_b))

    wi = bf(w_in)
    i1 = 3 * fox_w
    i2 = i1 + fox_heads
    i3 = i2 + 3 * diff_w
    wqk = jnp.concatenate([_pad_heads(wi[..., :fox_w], fox_heads, HEAD_DIM),
                           _pad_heads(wi[..., fox_w:2 * fox_w], fox_heads, HEAD_DIM)], axis=-1)
    wfvt = jnp.swapaxes(wi[..., 2 * fox_w:i1], 1, 2)
    wff = jnp.pad(wi[..., i1:i2], ((0, 0), (0, 0), (0, LANES - fox_heads)))
    fb = rows(jnp.pad(fox_f_bias, ((0, 0), (0, LANES - fox_heads))))
    wdqk = jnp.concatenate([_pad_maps(wi[..., i2:i2 + diff_w], diff_heads),
                            _pad_maps(wi[..., i2 + diff_w:i2 + 2 * diff_w], diff_heads)], axis=-1)
    wdvt = jnp.swapaxes(wi[..., i2 + 2 * diff_w:i3], 1, 2)
    wsu, wsg = wi[..., i3:i3 + sgu_w], wi[..., i3 + sgu_w:]
    bs_exp = jnp.repeat(jnp.swapaxes(sgu_b_s, 1, 2), SGU_HEAD_DIM, axis=2)
    lam = tuple(rows(v) for v in (diff_lambda_q1, diff_lambda_k1, diff_lambda_q2, diff_lambda_k2))
    gcol = diff_norm_g.reshape(depth, diff_heads, DIFF_V_DIM, 1)

    h = x
    for l in range(depth):
        lam_init = 0.8 - 0.6 * math.exp(-0.3 * l)
        h = _ffn_ln(h.reshape(bsz * seq, d), *wa, layer=l, alpha=alpha).reshape(bsz, seq, d)
        qa, ka, vft, q2, k2, vdt, ys = _mixer_in(
            h, wqk, wfvt, wff, fb, wdqk, wdvt, tri, eqk, ones, ext2,
            wsu, wsg, rows(sgu_norm_g), rows(sgu_norm_b), sgu_w_s, bs_exp,
            layer=l, fox_heads=fox_heads, diff_heads=diff_heads)
        yft = _fox_attention(qa, ka, vft)
        ydt = _diff_attention(q2, k2, vdt, *lam, gcol, layer=l, lam_init=lam_init)
        h = _out_ffn(yft, ydt, ys, h, *wo, *wb, layer=l, alpha=alpha)
    return h
```

```python
import functools
import math

import numpy as np
import jax
import jax.numpy as jnp
from jax import lax
from jax.experimental import pallas as pl
from jax.experimental.pallas import tpu as pltpu

F32 = jnp.float32
BF16 = jnp.bfloat16

HEAD_DIM = 64
DIFF_QK_DIM = 32
DIFF_V_DIM = 64
SGU_HEAD_DIM = 64
CHUNK = 128
LN_EPS = 1e-5
NEG_INF = -1e30
LOG2E = math.log2(math.e)
DV_AUG = HEAD_DIM + 16

LANES = 128
VMEM_LIMIT = 56 * 1024 * 1024

FOX_EXT = HEAD_DIM
DIFF_EXT = DIFF_QK_DIM

TQ = 512
TK = 256
MXU_DIM = 256
TM_FFN = 1024
FFN_SUB = 512
TM_IN = 512


def _nt_dot(a, b):
    return lax.dot_general(a, b, (((1,), (1,)), ((), ())), preferred_element_type=F32)


def _dot(a, b):
    return jnp.dot(a, b, preferred_element_type=F32)


def _layer_norm(y, g, b):
    mu = jnp.mean(y, axis=-1, keepdims=True)
    d = y - mu
    var = jnp.mean(d * d, axis=-1, keepdims=True)
    return d * lax.rsqrt(var + LN_EPS) * g + b


def _split3(x):
    hi = x.astype(BF16)
    r1 = x - hi.astype(F32)
    mid = r1.astype(BF16)
    r2 = r1 - mid.astype(F32)
    lo = r2.astype(BF16)
    return hi, mid, lo


def _split3_np(x):
    x = np.asarray(x, np.float32)
    hi = x.astype(BF16).astype(np.float32)
    r1 = x - hi
    mid = r1.astype(BF16).astype(np.float32)
    lo = (r1 - mid).astype(BF16).astype(np.float32)
    return hi, mid, lo


def _ffn_groups(load_rows, n_groups, sub, wg_ref, wu_ref, wd_ref, g_ref, b_ref, o_ref, *, alpha, chunks):
    x = load_rows(pl.ds(0, sub))
    finish = None
    for r in range(n_groups):
        xb = x.astype(BF16)
        acc = None
        x_next = None
        for ci, (c0, c1) in enumerate(chunks):
            gate = _dot(xb, wg_ref[:, c0:c1])
            up = _dot(xb, wu_ref[:, c0:c1])
            hidden = (gate * jax.nn.sigmoid(gate) * up).astype(BF16)
            part = _dot(hidden, wd_ref[c0:c1, :])
            acc = part if acc is None else acc + part
            if ci == 0:
                if finish is not None:
                    finish()
                if r + 1 < n_groups:
                    x_next = load_rows(pl.ds((r + 1) * sub, sub))
        y = alpha * x + 0.5 * acc

        def finish(y=y, rows=pl.ds(r * sub, sub)):
            o_ref[rows, :] = _layer_norm(y, g_ref[...], b_ref[...])
        x = x_next
    finish()


def _ffn_kernel(x_ref, wg_ref, wu_ref, wd_ref, g_ref, b_ref, o_ref, *, alpha, sub, chunks):
    _ffn_groups(lambda rows: x_ref[rows, :], x_ref.shape[0] // sub, sub,
                wg_ref, wu_ref, wd_ref, g_ref, b_ref, o_ref, alpha=alpha, chunks=chunks)


def _out_ffn_kernel(yf_ref, yd_ref, ys_ref, h_ref, wo_ref, gm_ref, bm_ref,
                    wg_ref, wu_ref, wd_ref, g_ref, b_ref, o_ref, *, alpha, sub, chunks):
    def mixed_rows(rows):
        yf = jnp.transpose(yf_ref[:, rows].astype(F32)).astype(BF16)
        yd = jnp.transpose(yd_ref[:, rows].astype(F32)).astype(BF16)
        y = jnp.concatenate([yf, yd, ys_ref[rows, :]], axis=1)
        return _layer_norm(alpha * h_ref[rows, :] + _dot(y, wo_ref[...]), gm_ref[...], bm_ref[...])

    _ffn_groups(mixed_rows, h_ref.shape[0] // sub, sub,
                wg_ref, wu_ref, wd_ref, g_ref, b_ref, o_ref, alpha=alpha, chunks=chunks)


def _resident(shape, layer=None):
    if layer is None:
        return pl.BlockSpec(shape, lambda *_: (0,) * len(shape), pipeline_mode=pl.Buffered(1))
    return pl.BlockSpec((None,) + tuple(shape[1:]), lambda *_: (layer,) + (0,) * (len(shape) - 1),
                        pipeline_mode=pl.Buffered(1))


def _hidden_chunks(f):
    half = (f // MXU_DIM + 1) // 2 * MXU_DIM
    return ((0, half), (half, f))


def _out_ffn(yft, ydt, ys, h, wo, gm, bm, wg, wu, wd, g, b, *, layer, alpha):
    bsz, s, d = h.shape
    fw, dw, sw = yft.shape[1], ydt.shape[1], ys.shape[2]
    tm = TM_FFN
    kern = functools.partial(_out_ffn_kernel, alpha=alpha, sub=FFN_SUB, chunks=_hidden_chunks(wg.shape[-1]))
    per_layer = lambda a: _resident(a.shape, layer)
    return pl.pallas_call(
        kern,
        grid=(bsz, s // tm),
        in_specs=[
            pl.BlockSpec((None, fw, tm), lambda bi, si: (bi, 0, si)),
            pl.BlockSpec((None, dw, tm), lambda bi, si: (bi, 0, si)),
            pl.BlockSpec((None, tm, sw), lambda bi, si: (bi, si, 0)),
            pl.BlockSpec((None, tm, d), lambda bi, si: (bi, si, 0)),
            per_layer(wo), per_layer(gm), per_layer(bm),
            per_layer(wg), per_layer(wu), per_layer(wd), per_layer(g), per_layer(b),
        ],
        out_specs=pl.BlockSpec((None, tm, d), lambda bi, si: (bi, si, 0)),
        out_shape=jax.ShapeDtypeStruct((bsz, s, d), F32),
        compiler_params=pltpu.CompilerParams(
            dimension_semantics=("parallel", "parallel"), vmem_limit_bytes=VMEM_LIMIT),
        name="out_ffn",
    )(yft, ydt, ys, h, wo, gm, bm, wg, wu, wd, g, b)


def _ffn_ln(x2d, wg, wu, wd, g, b, *, layer, alpha):
    t, d = x2d.shape
    tm = TM_FFN
    chunks = _hidden_chunks(wg.shape[-1])
    per_layer = lambda a: _resident(a.shape, layer)
    return pl.pallas_call(
        functools.partial(_ffn_kernel, alpha=alpha, sub=FFN_SUB, chunks=chunks),
        grid=(t // tm,),
        in_specs=[
            pl.BlockSpec((tm, d), lambda i: (i, 0)),
            per_layer(wg), per_layer(wu), per_layer(wd), per_layer(g), per_layer(b),
        ],
        out_specs=pl.BlockSpec((tm, d), lambda i: (i, 0)),
        out_shape=jax.ShapeDtypeStruct((t, d), F32),
        compiler_params=pltpu.CompilerParams(
            dimension_semantics=("parallel",), vmem_limit_bytes=VMEM_LIMIT),
        name="ffn_ln",
    )(x2d, wg, wu, wd, g, b)


def _mixer_in_kernel(h_ref, wqk_ref, wfvt_ref, wff_ref, fb_ref, wd_ref, wdvt_ref, tri_ref,
                     eqk_ref, ones_ref, ext2_ref, wsu_ref, wsg_ref, lg_ref, lb_ref, ws_ref, bs_ref,
                     qa_ref, ka_ref, vft_ref, q2_ref, k2_ref, vdt_ref, ys_ref, carry_ref,
                     *, fox_heads, diff_heads, diff_scale, fox_scale):
    s = pl.program_id(1)
    tm = h_ref.shape[0]
    hb = h_ref[...].astype(BF16)

    @pl.when(s == 0)
    def _():
        carry_ref[...] = jnp.zeros_like(carry_ref)

    fx = _dot(hb, wff_ref[...]) + fb_ref[...]
    nq = fox_heads * LANES
    pqk = _dot(hb, wqk_ref[...])

    log_f = -(jnp.maximum(-fx, 0.0) + jnp.log1p(jnp.exp(-jnp.abs(fx))))
    lane = lax.broadcasted_iota(jnp.int32, log_f.shape, 1)
    log_f = jnp.where(lane < fox_heads, log_f, 0.0)
    t3 = jnp.concatenate(_split3(log_f), axis=1)
    cs = _dot(tri_ref[...], t3)
    c = cs[:, :LANES] + cs[:, LANES:2 * LANES] + cs[:, 2 * LANES:] + carry_ref[...]
    carry_ref[...] = c[tm - 1:tm, :]
    c3 = jnp.concatenate(_split3(c * LOG2E), axis=1)
    ext = _dot(c3, eqk_ref[...])

    su = _dot(hb, wsu_ref[...])
    sg = _dot(hb, wsg_ref[...])

    for h in range(fox_heads):
        cq = slice(h * LANES, (h + 1) * LANES)
        ck = slice(nq + h * LANES, nq + (h + 1) * LANES)
        qa_ref[h] = (pqk[:, cq] * (fox_scale * LOG2E) + ext[:, :LANES] + ones_ref[:, cq]).astype(BF16)
        ka_ref[h] = (pqk[:, ck] + ext[:, LANES:] + ones_ref[:, ck]).astype(BF16)

    ones_rows = jnp.ones((DV_AUG - HEAD_DIM, TK), BF16)

    def store_values(vt, v_ref, heads):
        for cblk in range(tm // TK):
            for h in range(heads):
                v_ref[cblk, pl.ds(h * DV_AUG, HEAD_DIM), :] = (
                    vt[h * HEAD_DIM:(h + 1) * HEAD_DIM, cblk * TK:(cblk + 1) * TK])
                v_ref[cblk, pl.ds(h * DV_AUG + HEAD_DIM, DV_AUG - HEAD_DIM), :] = ones_rows

    store_values(_nt_dot(wfvt_ref[...], hb).astype(BF16), vft_ref, fox_heads)

    u = _gelu_tanh(su)
    g = _layer_norm(_gelu_tanh(sg), lg_ref[...], lb_ref[...]).astype(BF16)

    nd = diff_heads * LANES
    pd = _dot(hb, wd_ref[...])

    _sgu_mix(u, g, ws_ref, bs_ref, ys_ref)

    ext2 = ext2_ref[...].astype(F32)
    q2 = (pd[:, :nd] * (diff_scale * LOG2E) + ext2[:, :nd]).astype(BF16)
    k2 = (pd[:, nd:] + ext2[:, nd:]).astype(BF16)
    for h in range(diff_heads):
        q2_ref[h] = q2[:, h * LANES:(h + 1) * LANES]
        k2_ref[h] = k2[:, h * LANES:(h + 1) * LANES]

    store_values(_nt_dot(wdvt_ref[...], hb).astype(BF16), vdt_ref, diff_heads)


def _mixer_in(h, wqk, wfvt, wff, fb, wd, wdvt, tri, eqk, ones, ext2, wsu, wsg, lg, lb, ws, bs_exp,
              *, layer, fox_heads, diff_heads):
    b, s, d = h.shape
    tm = TM_IN
    nkv = s // TK
    fw = fox_heads * DV_AUG
    dw = diff_heads * DV_AUG
    const = _resident
    per_layer = lambda a: _resident(a.shape, layer)
    sgu_w = wsu.shape[-1]
    kern = functools.partial(
        _mixer_in_kernel, fox_heads=fox_heads, diff_heads=diff_heads,
        diff_scale=DIFF_QK_DIM ** -0.5, fox_scale=HEAD_DIM ** -0.5)
    return pl.pallas_call(
        kern,
        grid=(b, s // tm),
        in_specs=[
            pl.BlockSpec((None, tm, d), lambda bi, si: (bi, si, 0)),
            per_layer(wqk), per_layer(wfvt), per_layer(wff), per_layer(fb),
            per_layer(wd), per_layer(wdvt), const(tri.shape), const(eqk.shape),
            const(ones.shape),
            pl.BlockSpec((tm, ext2.shape[1]), lambda bi, si: (si, 0)),
            per_layer(wsu), per_layer(wsg), per_layer(lg), per_layer(lb),
            per_layer(ws), per_layer(bs_exp),
        ],
        out_specs=[
            pl.BlockSpec((None, fox_heads, tm, LANES), lambda bi, si: (bi, 0, si, 0)),
            pl.BlockSpec((None, fox_heads, tm, LANES), lambda bi, si: (bi, 0, si, 0)),
            pl.BlockSpec((None, tm // TK, fw, TK), lambda bi, si: (bi, si, 0, 0)),
            pl.BlockSpec((None, diff_heads, tm, LANES), lambda bi, si: (bi, 0, si, 0)),
            pl.BlockSpec((None, diff_heads, tm, LANES), lambda bi, si: (bi, 0, si, 0)),
            pl.BlockSpec((None, tm // TK, dw, TK), lambda bi, si: (bi, si, 0, 0)),
            pl.BlockSpec((None, tm, sgu_w), lambda bi, si: (bi, si, 0)),
        ],
        out_shape=[
            jax.ShapeDtypeStruct((b, fox_heads, s, LANES), BF16),
            jax.ShapeDtypeStruct((b, fox_heads, s, LANES), BF16),
            jax.ShapeDtypeStruct((b, nkv, fw, TK), BF16),
            jax.ShapeDtypeStruct((b, diff_heads, s, LANES), BF16),
            jax.ShapeDtypeStruct((b, diff_heads, s, LANES), BF16),
            jax.ShapeDtypeStruct((b, nkv, dw, TK), BF16),
            jax.ShapeDtypeStruct((b, s, sgu_w), BF16),
        ],
        scratch_shapes=[pltpu.VMEM((1, LANES), F32)],
        compiler_params=pltpu.CompilerParams(
            dimension_semantics=("parallel", "arbitrary"), vmem_limit_bytes=VMEM_LIMIT),
        name="mixer_in",
    )(h, wqk, wfvt, wff, fb, wd, wdvt, tri, eqk, ones, ext2, wsu, wsg, lg, lb, ws, bs_exp)


def _gelu_tanh(x):
    c = math.sqrt(2.0 / math.pi)
    return 0.5 * x * (1.0 + jnp.tanh(c * (x + 0.044715 * (x * x * x))))


def _sgu_mix(u, g, ws_ref, bs_ref, o_ref):
    tm = u.shape[0]
    heads = ws_ref.shape[0]
    row = lax.broadcasted_iota(jnp.int32, (CHUNK, CHUNK), 0)
    col = lax.broadcasted_iota(jnp.int32, (CHUNK, CHUNK), 1)
    causal = col <= row
    w = [jnp.where(causal, ws_ref[hh], 0.0).astype(BF16) for hh in range(heads)]
    lane = lax.broadcasted_iota(jnp.int32, (CHUNK, LANES), 1)
    low = lane < SGU_HEAD_DIM
    bias = bs_ref[...]
    for cblk in range(tm // CHUNK):
        rows = slice(cblk * CHUNK, (cblk + 1) * CHUNK)
        for p in range(heads // 2):
            cols = slice(p * LANES, (p + 1) * LANES)
            gp = g[rows, cols]
            zero = jnp.zeros_like(gp)
            mixed = (_dot(w[2 * p], jnp.where(low, gp, zero))
                     + _dot(w[2 * p + 1], jnp.where(low, zero, gp))
                     + bias[:, cols])
            o_ref[rows, cols] = (u[rows, cols] * mixed).astype(o_ref.dtype)


def _flash_sweep(qts, k_refs, v_refs, qi, scratch):
    n = len(qts)
    assert TQ == 2 * TK
    s_refs, bm_refs, m_ref, acc_ref, qt_ref = scratch[0:2], scratch[2:4], scratch[4], scratch[5], scratch[6]

    def scores(i, j, slot):
        start = pl.multiple_of(j * TK, TK)
        st = _dot(k_refs[i][pl.ds(start, TK), :], qt_ref[i])
        s_refs[slot][i] = st
        bm_refs[slot][i] = jnp.max(st, axis=0, keepdims=True)

    def step(i, j, slot):
        scores(i, j + 1, 1 - slot)
        m = m_ref[i]
        m_new = jnp.maximum(m, bm_refs[slot][i])
        p = jnp.exp2(s_refs[slot][i] - m_new).astype(BF16)
        m_ref[i] = m_new
        acc_ref[i] = acc_ref[i] * jnp.exp2(m - m_new) + _dot(v_refs[i][j], p)

    for i in range(n):
        qt_ref[i] = qts[i]()
        scores(i, 0, 0)
        m_ref[i] = jnp.full(m_ref.shape[1:], NEG_INF, F32)
        acc_ref[i] = jnp.zeros(acc_ref.shape[1:], F32)

    def pair(first):
        for i in range(n):
            step(i, first, 0)
        for i in range(n):
            step(i, first + 1, 1)

    @pl.loop(0, lax.shift_right_logical(qi, 1))
    def _(u):
        pair(4 * u)
        pair(4 * u + 2)

    @pl.when(jnp.bitwise_and(qi, 1) == 1)
    def _():
        pair(2 * qi - 2)

    causal = (lax.broadcasted_iota(jnp.int32, (TK, TK), 0) <= lax.broadcasted_iota(jnp.int32, (TK, TK), 1))
    last = 2 * qi + 1

    def first_diagonal(i):
        start = pl.multiple_of(last * TK, TK)
        s_refs[1][i, :, TK:] = _dot(k_refs[i][pl.ds(start, TK), :], qt_ref[i, :, TK:])
        st = s_refs[0][i]
        st = jnp.concatenate([jnp.where(causal, st[:, :TK], NEG_INF), st[:, TK:]], axis=1)
        m = m_ref[i]
        m_new = jnp.maximum(m, jnp.max(st, axis=0, keepdims=True))
        p = jnp.exp2(st - m_new).astype(BF16)
        m_ref[i] = m_new
        acc_ref[i] = acc_ref[i] * jnp.exp2(m - m_new) + _dot(v_refs[i][last - 1], p)

    def second_diagonal(i):
        st = jnp.where(causal, s_refs[1][i, :, TK:], NEG_INF)
        m = m_ref[i, :, TK:]
        m_new = jnp.maximum(m, jnp.max(st, axis=0, keepdims=True))
        p = jnp.exp2(st - m_new).astype(BF16)
        acc_ref[i, :, TK:] = acc_ref[i, :, TK:] * jnp.exp2(m - m_new) + _dot(v_refs[i][last], p)

    for i in range(n):
        first_diagonal(i)
    for i in range(n):
        second_diagonal(i)


ACC_SCRATCH = 5


def _sweep_scratch(n):
    return ([pltpu.VMEM((n, TK, TQ), F32)] * 2 + [pltpu.VMEM((n, 1, TQ), F32)] * 2
            + [pltpu.VMEM((n, 1, TQ), F32), pltpu.VMEM((n, DV_AUG, TQ), F32),
               pltpu.VMEM((n, LANES, TQ), BF16)])


def _transposed(q):
    return jnp.transpose(q)


def _normalised(acc_ref, i):
    return acc_ref[i, :HEAD_DIM, :] / acc_ref[i, HEAD_DIM:HEAD_DIM + 1, :]


def _fox_kernel(q_ref, k_ref, v_ref, o_ref, *scratch, hp):
    qi = pl.program_id(2)
    qts = [functools.partial(lambda hh: _transposed(q_ref[hh]), hh) for hh in range(hp)]
    k_refs = [k_ref.at[hh] for hh in range(hp)]
    v_refs = [v_ref.at[:, pl.ds(hh * DV_AUG, DV_AUG), :] for hh in range(hp)]
    _flash_sweep(qts, k_refs, v_refs, qi, scratch)
    for hh in range(hp):
        o_ref[pl.ds(hh * HEAD_DIM, HEAD_DIM), :] = _normalised(scratch[ACC_SCRATCH], hh).astype(o_ref.dtype)


def _fox_attention(qa, ka, vft, *, hp=6):
    b, heads, s, _ = qa.shape
    nkv = vft.shape[1]
    return pl.pallas_call(
        functools.partial(_fox_kernel, hp=hp),
        grid=(b, heads // hp, s // TQ),
        in_specs=[
            pl.BlockSpec((None, hp, TQ, LANES), lambda bi, hi, qi: (bi, hi, qi, 0)),
            pl.BlockSpec((None, hp, s, LANES), lambda bi, hi, qi: (bi, hi, 0, 0)),
            pl.BlockSpec((None, nkv, hp * DV_AUG, TK), lambda bi, hi, qi: (bi, 0, hi, 0)),
        ],
        out_specs=pl.BlockSpec((None, hp * HEAD_DIM, TQ), lambda bi, hi, qi: (bi, hi, qi)),
        out_shape=jax.ShapeDtypeStruct((b, heads * HEAD_DIM, s), BF16),
        scratch_shapes=_sweep_scratch(hp),
        compiler_params=pltpu.CompilerParams(
            dimension_semantics=("parallel", "parallel", "arbitrary"),
            vmem_limit_bytes=VMEM_LIMIT),
        name="fox_attention",
    )(qa, ka, vft)


def _diff_kernel(q_ref, k_ref, v_ref, lq1_ref, lk1_ref, lq2_ref, lk2_ref, g_ref, o_ref,
                 *scratch, hp, lam_init):
    qi = pl.program_id(2)
    feature = lax.broadcasted_iota(jnp.int32, (LANES, TQ), 0)
    first_map = feature < 2 * DIFF_QK_DIM
    qts, k_refs, v_refs = [], [], []

    transposed = {}

    def one_map(hh, first):
        if hh not in transposed:
            transposed[hh] = _transposed(q_ref[hh])
        qt = transposed[hh]
        zero = jnp.zeros_like(qt)
        return jnp.where(first_map, qt, zero) if first else jnp.where(first_map, zero, qt)

    for hh in range(hp):
        qts += [functools.partial(one_map, hh, True), functools.partial(one_map, hh, False)]
        k_refs += [k_ref.at[hh]] * 2
        v_refs += [v_ref.at[:, pl.ds(hh * DV_AUG, DV_AUG), :]] * 2
    _flash_sweep(qts, k_refs, v_refs, qi, scratch)

    lam = (jnp.exp(jnp.sum(lq1_ref[...] * lk1_ref[...], axis=-1, keepdims=True))
           - jnp.exp(jnp.sum(lq2_ref[...] * lk2_ref[...], axis=-1, keepdims=True))
           + lam_init)
    for hh in range(hp):
        o = _normalised(scratch[ACC_SCRATCH], 2 * hh) - lam * _normalised(scratch[ACC_SCRATCH], 2 * hh + 1)
        ms = jnp.mean(o * o, axis=0, keepdims=True)
        o = o * lax.rsqrt(ms + LN_EPS)
        o = o * g_ref[hh] * (1.0 - lam_init)
        o_ref[pl.ds(hh * DIFF_V_DIM, DIFF_V_DIM), :] = o.astype(o_ref.dtype)


def _diff_attention(q2, k2, vdt, lq1, lk1, lq2, lk2, gcol, *, layer, lam_init, hp=4):
    b, heads, s, _ = q2.shape
    nkv = vdt.shape[1]
    vec = lambda a: pl.BlockSpec((None,) + a.shape[1:], lambda bi, hi, qi: (layer, 0, 0))
    return pl.pallas_call(
        functools.partial(_diff_kernel, hp=hp, lam_init=lam_init),
        grid=(b, heads // hp, s // TQ),
        in_specs=[
            pl.BlockSpec((None, hp, TQ, LANES), lambda bi, hi, qi: (bi, hi, qi, 0)),
            pl.BlockSpec((None, hp, s, LANES), lambda bi, hi, qi: (bi, hi, 0, 0)),
            pl.BlockSpec((None, nkv, hp * DV_AUG, TK), lambda bi, hi, qi: (bi, 0, hi, 0)),
            vec(lq1), vec(lk1), vec(lq2), vec(lk2),
            pl.BlockSpec((None, hp, DIFF_V_DIM, 1), lambda bi, hi, qi: (layer, hi, 0, 0)),
        ],
        out_specs=pl.BlockSpec((None, hp * DIFF_V_DIM, TQ), lambda bi, hi, qi: (bi, hi, qi)),
        out_shape=jax.ShapeDtypeStruct((b, heads * DIFF_V_DIM, s), BF16),
        scratch_shapes=_sweep_scratch(2 * hp),
        compiler_params=pltpu.CompilerParams(
            dimension_semantics=("parallel", "parallel", "arbitrary"),
            vmem_limit_bytes=VMEM_LIMIT),
        name="diff_attention",
    )(q2, k2, vdt, lq1, lk1, lq2, lk2, gcol)


def _fox_bias_tables(fox_heads):
    nq = fox_heads * LANES
    nterm = 3 * fox_heads
    assert FOX_EXT + 2 * nterm <= LANES
    eqk = np.zeros((3 * LANES, 2 * LANES), np.float32)
    ones = np.zeros((1, 2 * nq), np.float32)
    for h in range(fox_heads):
        for t in range(3):
            a = FOX_EXT + fox_heads * t + h
            eqk[t * LANES + h, a + nterm] = 1.0
            eqk[t * LANES + h, LANES + a] = -1.0
            ones[0, h * LANES + a] = 1.0
            ones[0, nq + h * LANES + a + nterm] = 1.0
    return jnp.asarray(eqk, BF16), jnp.asarray(ones, F32)


def _alibi_tables(seq, diff_heads):
    nd = diff_heads * LANES
    ext = np.zeros((seq, 2 * nd), np.float32)
    pos = np.arange(seq, dtype=np.float32)
    slopes = np.float32(2.0) ** (np.float32(-8.0) * np.arange(1, diff_heads + 1, dtype=np.float32)
                                 / np.float32(diff_heads))
    for h in range(diff_heads):
        q_terms = _split3_np(np.float32(LOG2E) * (-slopes[h] * pos))
        k_terms = _split3_np(np.float32(LOG2E) * (slopes[h] * pos))
        for mp in range(2):
            base = h * LANES + mp * 2 * DIFF_QK_DIM + DIFF_EXT
            for t in range(3):
                ext[:, base + t] = 1.0
                ext[:, base + 3 + t] = q_terms[t]
                ext[:, nd + base + t] = k_terms[t]
                ext[:, nd + base + 3 + t] = 1.0
    return jnp.asarray(ext, BF16)


def _pad_heads(w, heads, dim):
    lead = w.shape[:-1]
    w = w.reshape(lead + (heads, dim))
    return jnp.pad(w, ((0, 0),) * (len(lead) + 1) + ((0, LANES - dim),)).reshape(lead + (heads * LANES,))


def _pad_maps(w, heads):
    lead = w.shape[:-1]
    w = w.reshape(lead + (heads, 2, DIFF_QK_DIM))
    w = jnp.pad(w, ((0, 0),) * (len(lead) + 2) + ((0, DIFF_QK_DIM),))
    return w.reshape(lead + (heads * LANES,))


def kernel(x, ffn_a_w_gate, ffn_a_w_up, ffn_a_w_down, norm_a_g, norm_a_b, w_in, fox_f_bias, diff_lambda_q1, diff_lambda_k1, diff_lambda_q2, diff_lambda_k2, diff_norm_g, sgu_norm_g, sgu_norm_b, sgu_w_s, sgu_b_s, w_out, norm_m_g, norm_m_b, ffn_b_w_gate, ffn_b_w_up, ffn_b_w_down, norm_b_g, norm_b_b):
    bsz, seq, d = x.shape
    depth = w_in.shape[0]
    fox_heads = fox_f_bias.shape[1]
    fox_w = fox_heads * HEAD_DIM
    diff_w = diff_norm_g.shape[1]
    diff_heads = diff_w // DIFF_V_DIM
    sgu_w = sgu_norm_g.shape[1]
    alpha = (2 * depth) ** 0.25

    eqk, ones = _fox_bias_tables(fox_heads)
    ext2 = _alibi_tables(seq, diff_heads)
    tri = jnp.asarray(np.tril(np.ones((TM_IN, TM_IN), np.float32)), BF16)

    bf = lambda w: w.astype(BF16)
    rows = lambda v: v.reshape(depth, 1, -1)
    wa = (bf(ffn_a_w_gate), bf(ffn_a_w_up), bf(ffn_a_w_down), rows(norm_a_g), rows(norm_a_b))
    wb = (bf(ffn_b_w_gate), bf(ffn_b_w_up), bf(ffn_b_w_down), rows(norm_b_g), rows(norm_b_b))
    wo = (bf(w_out), rows(norm_m_g), rows(norm_m_b))

    wi = bf(w_in)
    i1 = 3 * fox_w
    i2 = i1 + fox_heads
    i3 = i2 + 3 * diff_w
    wqk = jnp.concatenate([_pad_heads(wi[..., :fox_w], fox_heads, HEAD_DIM),
                           _pad_heads(wi[..., fox_w:2 * fox_w], fox_heads, HEAD_DIM)], axis=-1)
    wfvt = jnp.swapaxes(wi[..., 2 * fox_w:i1], 1, 2)
    wff = jnp.pad(wi[..., i1:i2], ((0, 0), (0, 0), (0, LANES - fox_heads)))
    fb = rows(jnp.pad(fox_f_bias, ((0, 0), (0, LANES - fox_heads))))
    wdqk = jnp.concatenate([_pad_maps(wi[..., i2:i2 + diff_w], diff_heads),
                            _pad_maps(wi[..., i2 + diff_w:i2 + 2 * diff_w], diff_heads)], axis=-1)
    wdvt = jnp.swapaxes(wi[..., i2 + 2 * diff_w:i3], 1, 2)
    wsu, wsg = wi[..., i3:i3 + sgu_w], wi[..., i3 + sgu_w:]
    bs_exp = jnp.repeat(jnp.swapaxes(sgu_b_s, 1, 2), SGU_HEAD_DIM, axis=2)
    lam = tuple(rows(v) for v in (diff_lambda_q1, diff_lambda_k1, diff_lambda_q2, diff_lambda_k2))
    gcol = diff_norm_g.reshape(depth, diff_heads, DIFF_V_DIM, 1)

    h = x
    for l in range(depth):
        lam_init = 0.8 - 0.6 * math.exp(-0.3 * l)
        h = _ffn_ln(h.reshape(bsz * seq, d), *wa, layer=l, alpha=alpha).reshape(bsz, seq, d)
        qa, ka, vft, q2, k2, vdt, ys = _mixer_in(
            h, wqk, wfvt, wff, fb, wdqk, wdvt, tri, eqk, ones, ext2,
            wsu, wsg, rows(sgu_norm_g), rows(sgu_norm_b), sgu_w_s, bs_exp,
            layer=l, fox_heads=fox_heads, diff_heads=diff_heads)
        yft = _fox_attention(qa, ka, vft)
        ydt = _diff_attention(q2, k2, vdt, *lam, gcol, layer=l, lam_init=lam_init)
        h = _out_ffn(yft, ydt, ys, h, *wo, *wb, layer=l, alpha=alpha)
    return h
```

```python
import functools
import math

import numpy as np
import jax
import jax.numpy as jnp
from jax import lax
from jax.experimental import pallas as pl
from jax.experimental.pallas import tpu as pltpu

F32 = jnp.float32
BF16 = jnp.bfloat16

HEAD_DIM = 64
DIFF_QK_DIM = 32
DIFF_V_DIM = 64
SGU_HEAD_DIM = 64
CHUNK = 128
LN_EPS = 1e-5
NEG_INF = -1e30
LOG2E = math.log2(math.e)
DV_AUG = HEAD_DIM + 16

LANES = 128
VMEM_LIMIT = 56 * 1024 * 1024

FOX_EXT = HEAD_DIM
DIFF_EXT = DIFF_QK_DIM
GATE_LANE = 40

TQ = 512
TK = 256
MXU_DIM = 256
TM_FFN = 1024
FFN_SUB = 512
TM_IN = 512


def _nt_dot(a, b):
    return lax.dot_general(a, b, (((1,), (1,)), ((), ())), preferred_element_type=F32)


def _dot(a, b):
    return jnp.dot(a, b, preferred_element_type=F32)


def _layer_norm(y, g, b):
    mu = jnp.mean(y, axis=-1, keepdims=True)
    d = y - mu
    var = jnp.mean(d * d, axis=-1, keepdims=True)
    return d * lax.rsqrt(var + LN_EPS) * g + b


def _split3(x):
    hi = x.astype(BF16)
    r1 = x - hi.astype(F32)
    mid = r1.astype(BF16)
    r2 = r1 - mid.astype(F32)
    lo = r2.astype(BF16)
    return hi, mid, lo


def _split3_np(x):
    x = np.asarray(x, np.float32)
    hi = x.astype(BF16).astype(np.float32)
    r1 = x - hi
    mid = r1.astype(BF16).astype(np.float32)
    lo = (r1 - mid).astype(BF16).astype(np.float32)
    return hi, mid, lo


def _ffn_groups(load_rows, n_groups, sub, wg_ref, wu_ref, wd_ref, g_ref, b_ref, o_ref, *, alpha, chunks):
    x = load_rows(pl.ds(0, sub))
    finish = None
    for r in range(n_groups):
        xb = x.astype(BF16)
        acc = None
        x_next = None
        for ci, (c0, c1) in enumerate(chunks):
            gate = _dot(xb, wg_ref[:, c0:c1])
            up = _dot(xb, wu_ref[:, c0:c1])
            hidden = (gate * jax.nn.sigmoid(gate) * up).astype(BF16)
            part = _dot(hidden, wd_ref[c0:c1, :])
            acc = part if acc is None else acc + part
            if ci == 0:
                if finish is not None:
                    finish()
                if r + 1 < n_groups:
                    x_next = load_rows(pl.ds((r + 1) * sub, sub))
        y = alpha * x + 0.5 * acc

        def finish(y=y, rows=pl.ds(r * sub, sub)):
            o_ref[rows, :] = _layer_norm(y, g_ref[...], b_ref[...])
        x = x_next
    finish()


def _ffn_kernel(x_ref, wg_ref, wu_ref, wd_ref, g_ref, b_ref, o_ref, *, alpha, sub, chunks):
    _ffn_groups(lambda rows: x_ref[rows, :], x_ref.shape[0] // sub, sub,
                wg_ref, wu_ref, wd_ref, g_ref, b_ref, o_ref, alpha=alpha, chunks=chunks)


def _out_ffn_kernel(yf_ref, yd_ref, ys_ref, h_ref, wo_ref, gm_ref, bm_ref,
                    wg_ref, wu_ref, wd_ref, g_ref, b_ref, o_ref, *, alpha, sub, chunks):
    def mixed_rows(rows):
        yf = jnp.transpose(yf_ref[:, rows].astype(F32)).astype(BF16)
        yd = jnp.transpose(yd_ref[:, rows].astype(F32)).astype(BF16)
        y = jnp.concatenate([yf, yd, ys_ref[rows, :]], axis=1)
        return _layer_norm(alpha * h_ref[rows, :] + _dot(y, wo_ref[...]), gm_ref[...], bm_ref[...])

    _ffn_groups(mixed_rows, h_ref.shape[0] // sub, sub,
                wg_ref, wu_ref, wd_ref, g_ref, b_ref, o_ref, alpha=alpha, chunks=chunks)


def _resident(shape, layer=None):
    if layer is None:
        return pl.BlockSpec(shape, lambda *_: (0,) * len(shape), pipeline_mode=pl.Buffered(1))
    return pl.BlockSpec((None,) + tuple(shape[1:]), lambda *_: (layer,) + (0,) * (len(shape) - 1),
                        pipeline_mode=pl.Buffered(1))


def _hidden_chunks(f):
    half = (f // MXU_DIM + 1) // 2 * MXU_DIM
    return ((0, half), (half, f))


def _out_ffn(yft, ydt, ys, h, wo, gm, bm, wg, wu, wd, g, b, *, layer, alpha):
    bsz, s, d = h.shape
    fw, dw, sw = yft.shape[1], ydt.shape[1], ys.shape[2]
    tm = TM_FFN
    kern = functools.partial(_out_ffn_kernel, alpha=alpha, sub=FFN_SUB, chunks=_hidden_chunks(wg.shape[-1]))
    per_layer = lambda a: _resident(a.shape, layer)
    return pl.pallas_call(
        kern,
        grid=(bsz, s // tm),
        in_specs=[
            pl.BlockSpec((None, fw, tm), lambda bi, si: (bi, 0, si)),
            pl.BlockSpec((None, dw, tm), lambda bi, si: (bi, 0, si)),
            pl.BlockSpec((None, tm, sw), lambda bi, si: (bi, si, 0)),
            pl.BlockSpec((None, tm, d), lambda bi, si: (bi, si, 0)),
            per_layer(wo), per_layer(gm), per_layer(bm),
            per_layer(wg), per_layer(wu), per_layer(wd), per_layer(g), per_layer(b),
        ],
        out_specs=pl.BlockSpec((None, tm, d), lambda bi, si: (bi, si, 0)),
        out_shape=jax.ShapeDtypeStruct((bsz, s, d), F32),
        compiler_params=pltpu.CompilerParams(
            dimension_semantics=("parallel", "parallel"), vmem_limit_bytes=VMEM_LIMIT),
        name="out_ffn",
    )(yft, ydt, ys, h, wo, gm, bm, wg, wu, wd, g, b)


def _ffn_ln(x2d, wg, wu, wd, g, b, *, layer, alpha):
    t, d = x2d.shape
    tm = TM_FFN
    chunks = _hidden_chunks(wg.shape[-1])
    per_layer = lambda a: _resident(a.shape, layer)
    return pl.pallas_call(
        functools.partial(_ffn_kernel, alpha=alpha, sub=FFN_SUB, chunks=chunks),
        grid=(t // tm,),
        in_specs=[
            pl.BlockSpec((tm, d), lambda i: (i, 0)),
            per_layer(wg), per_layer(wu), per_layer(wd), per_layer(g), per_layer(b),
        ],
        out_specs=pl.BlockSpec((tm, d), lambda i: (i, 0)),
        out_shape=jax.ShapeDtypeStruct((t, d), F32),
        compiler_params=pltpu.CompilerParams(
            dimension_semantics=("parallel",), vmem_limit_bytes=VMEM_LIMIT),
        name="ffn_ln",
    )(x2d, wg, wu, wd, g, b)


def _mixer_in_kernel(h_ref, wqk_ref, wfvt_ref, fb_ref, wd_ref, wdvt_ref, tri_ref,
                     eqk_ref, ones_ref, ext2_ref, wsu_ref, wsg_ref, lg_ref, lb_ref, ws_ref, bs_ref,
                     qa_ref, ka_ref, vft_ref, q2_ref, k2_ref, vdt_ref, ys_ref, carry_ref,
                     *, fox_heads, diff_heads, diff_scale, fox_scale):
    s = pl.program_id(1)
    tm = h_ref.shape[0]
    hb = h_ref[...].astype(BF16)

    @pl.when(s == 0)
    def _():
        carry_ref[...] = jnp.zeros_like(carry_ref)

    nd = diff_heads * LANES
    pd = _dot(hb, wd_ref[...])
    fx = pd[:, nd:nd + LANES] + fb_ref[...]
    nq = fox_heads * LANES
    pqk = _dot(hb, wqk_ref[...])

    log_f = -(jnp.maximum(-fx, 0.0) + jnp.log1p(jnp.exp(-jnp.abs(fx))))
    lane = lax.broadcasted_iota(jnp.int32, log_f.shape, 1)
    gate_lanes = (lane >= GATE_LANE) & (lane < GATE_LANE + fox_heads)
    log_f = jnp.where(gate_lanes, log_f, 0.0)
    t3 = jnp.concatenate(_split3(log_f), axis=1)
    cs = _dot(tri_ref[...], t3)
    c = cs[:, :LANES] + cs[:, LANES:2 * LANES] + cs[:, 2 * LANES:] + carry_ref[...]
    carry_ref[...] = c[tm - 1:tm, :]
    c3 = jnp.concatenate(_split3(c * LOG2E), axis=1)
    ext = _dot(c3, eqk_ref[...])

    su = _dot(hb, wsu_ref[...])
    sg = _dot(hb, wsg_ref[...])

    for h in range(fox_heads):
        cq = slice(h * LANES, (h + 1) * LANES)
        ck = slice(nq + h * LANES, nq + (h + 1) * LANES)
        qa_ref[h] = (pqk[:, cq] * (fox_scale * LOG2E) + ext[:, :LANES] + ones_ref[:, cq]).astype(BF16)
        ka_ref[h] = (pqk[:, ck] + ext[:, LANES:] + ones_ref[:, ck]).astype(BF16)

    ones_rows = jnp.ones((DV_AUG - HEAD_DIM, TK), BF16)

    def store_values(vt, v_ref, heads):
        for cblk in range(tm // TK):
            for h in range(heads):
                v_ref[cblk, pl.ds(h * DV_AUG, HEAD_DIM), :] = (
                    vt[h * HEAD_DIM:(h + 1) * HEAD_DIM, cblk * TK:(cblk + 1) * TK])
                v_ref[cblk, pl.ds(h * DV_AUG + HEAD_DIM, DV_AUG - HEAD_DIM), :] = ones_rows

    store_values(_nt_dot(wfvt_ref[...], hb).astype(BF16), vft_ref, fox_heads)

    u = _gelu_tanh(su)
    g = _layer_norm(_gelu_tanh(sg), lg_ref[...], lb_ref[...]).astype(BF16)

    _sgu_mix(u, g, ws_ref, bs_ref, ys_ref)

    ext2 = ext2_ref[...].astype(F32)
    q2 = (pd[:, :nd] * (diff_scale * LOG2E) + ext2[:, :nd]).astype(BF16)
    k2 = (pd[:, nd:] + ext2[:, nd:]).astype(BF16)
    for h in range(diff_heads):
        q2_ref[h] = q2[:, h * LANES:(h + 1) * LANES]
        kh = k2[:, h * LANES:(h + 1) * LANES]
        k2_ref[h] = jnp.where(gate_lanes, jnp.zeros_like(kh), kh) if h == 0 else kh

    store_values(_nt_dot(wdvt_ref[...], hb).astype(BF16), vdt_ref, diff_heads)


def _mixer_in(h, wqk, wfvt, fb, wd, wdvt, tri, eqk, ones, ext2, wsu, wsg, lg, lb, ws, bs_exp,
              *, layer, fox_heads, diff_heads):
    b, s, d = h.shape
    tm = TM_IN
    nkv = s // TK
    fw = fox_heads * DV_AUG
    dw = diff_heads * DV_AUG
    const = _resident
    per_layer = lambda a: _resident(a.shape, layer)
    sgu_w = wsu.shape[-1]
    kern = functools.partial(
        _mixer_in_kernel, fox_heads=fox_heads, diff_heads=diff_heads,
        diff_scale=DIFF_QK_DIM ** -0.5, fox_scale=HEAD_DIM ** -0.5)
    return pl.pallas_call(
        kern,
        grid=(b, s // tm),
        in_specs=[
            pl.BlockSpec((None, tm, d), lambda bi, si: (bi, si, 0)),
            per_layer(wqk), per_layer(wfvt), per_layer(fb),
            per_layer(wd), per_layer(wdvt), const(tri.shape), const(eqk.shape),
            const(ones.shape),
            pl.BlockSpec((tm, ext2.shape[1]), lambda bi, si: (si, 0)),
            per_layer(wsu), per_layer(wsg), per_layer(lg), per_layer(lb),
            per_layer(ws), per_layer(bs_exp),
        ],
        out_specs=[
            pl.BlockSpec((None, fox_heads, tm, LANES), lambda bi, si: (bi, 0, si, 0)),
            pl.BlockSpec((None, fox_heads, tm, LANES), lambda bi, si: (bi, 0, si, 0)),
            pl.BlockSpec((None, tm // TK, fw, TK), lambda bi, si: (bi, si, 0, 0)),
            pl.BlockSpec((None, diff_heads, tm, LANES), lambda bi, si: (bi, 0, si, 0)),
            pl.BlockSpec((None, diff_heads, tm, LANES), lambda bi, si: (bi, 0, si, 0)),
            pl.BlockSpec((None, tm // TK, dw, TK), lambda bi, si: (bi, si, 0, 0)),
            pl.BlockSpec((None, tm, sgu_w), lambda bi, si: (bi, si, 0)),
        ],
        out_shape=[
            jax.ShapeDtypeStruct((b, fox_heads, s, LANES), BF16),
            jax.ShapeDtypeStruct((b, fox_heads, s, LANES), BF16),
            jax.ShapeDtypeStruct((b, nkv, fw, TK), BF16),
            jax.ShapeDtypeStruct((b, diff_heads, s, LANES), BF16),
            jax.ShapeDtypeStruct((b, diff_heads, s, LANES), BF16),
            jax.ShapeDtypeStruct((b, nkv, dw, TK), BF16),
            jax.ShapeDtypeStruct((b, s, sgu_w), BF16),
        ],
        scratch_shapes=[pltpu.VMEM((1, LANES), F32)],
        compiler_params=pltpu.CompilerParams(
            dimension_semantics=("parallel", "arbitrary"), vmem_limit_bytes=VMEM_LIMIT),
        name="mixer_in",
    )(h, wqk, wfvt, fb, wd, wdvt, tri, eqk, ones, ext2, wsu, wsg, lg, lb, ws, bs_exp)


def _gelu_tanh(x):
    c = math.sqrt(2.0 / math.pi)
    return 0.5 * x * (1.0 + jnp.tanh(c * (x + 0.044715 * (x * x * x))))


def _sgu_mix(u, g, ws_ref, bs_ref, o_ref):
    tm = u.shape[0]
    heads = ws_ref.shape[0]
    row = lax.broadcasted_iota(jnp.int32, (CHUNK, CHUNK), 0)
    col = lax.broadcasted_iota(jnp.int32, (CHUNK, CHUNK), 1)
    causal = col <= row
    w = [jnp.where(causal, ws_ref[hh], 0.0).astype(BF16) for hh in range(heads)]
    lane = lax.broadcasted_iota(jnp.int32, (CHUNK, LANES), 1)
    low = lane < SGU_HEAD_DIM
    bias = bs_ref[...]
    for cblk in range(tm // CHUNK):
        rows = slice(cblk * CHUNK, (cblk + 1) * CHUNK)
        for p in range(heads // 2):
            cols = slice(p * LANES, (p + 1) * LANES)
            gp = g[rows, cols]
            zero = jnp.zeros_like(gp)
            mixed = (_dot(w[2 * p], jnp.where(low, gp, zero))
                     + _dot(w[2 * p + 1], jnp.where(low, zero, gp))
                     + bias[:, cols])
            o_ref[rows, cols] = (u[rows, cols] * mixed).astype(o_ref.dtype)


def _flash_sweep(qts, k_refs, v_refs, qi, scratch):
    n = len(qts)
    assert TQ == 2 * TK
    s_refs, bm_refs, m_ref, acc_ref, qt_ref = scratch[0:2], scratch[2:4], scratch[4], scratch[5], scratch[6]

    def scores(i, j, slot):
        start = pl.multiple_of(j * TK, TK)
        st = _dot(k_refs[i][pl.ds(start, TK), :], qt_ref[i])
        s_refs[slot][i] = st
        bm_refs[slot][i] = jnp.max(st, axis=0, keepdims=True)

    def step(i, j, slot):
        scores(i, j + 1, 1 - slot)
        m = m_ref[i]
        m_new = jnp.maximum(m, bm_refs[slot][i])
        p = jnp.exp2(s_refs[slot][i] - m_new).astype(BF16)
        m_ref[i] = m_new
        acc_ref[i] = acc_ref[i] * jnp.exp2(m - m_new) + _dot(v_refs[i][j], p)

    for i in range(n):
        qt_ref[i] = qts[i]()
        scores(i, 0, 0)
        m_ref[i] = jnp.full(m_ref.shape[1:], NEG_INF, F32)
        acc_ref[i] = jnp.zeros(acc_ref.shape[1:], F32)

    def pair(first):
        for i in range(n):
            step(i, first, 0)
        for i in range(n):
            step(i, first + 1, 1)

    @pl.loop(0, lax.shift_right_logical(qi, 1))
    def _(u):
        pair(4 * u)
        pair(4 * u + 2)

    @pl.when(jnp.bitwise_and(qi, 1) == 1)
    def _():
        pair(2 * qi - 2)

    causal = (lax.broadcasted_iota(jnp.int32, (TK, TK), 0) <= lax.broadcasted_iota(jnp.int32, (TK, TK), 1))
    last = 2 * qi + 1

    def first_diagonal(i):
        start = pl.multiple_of(last * TK, TK)
        s_refs[1][i, :, TK:] = _dot(k_refs[i][pl.ds(start, TK), :], qt_ref[i, :, TK:])
        st = s_refs[0][i]
        st = jnp.concatenate([jnp.where(causal, st[:, :TK], NEG_INF), st[:, TK:]], axis=1)
        m = m_ref[i]
        m_new = jnp.maximum(m, jnp.max(st, axis=0, keepdims=True))
        p = jnp.exp2(st - m_new).astype(BF16)
        m_ref[i] = m_new
        acc_ref[i] = acc_ref[i] * jnp.exp2(m - m_new) + _dot(v_refs[i][last - 1], p)

    def second_diagonal(i):
        st = jnp.where(causal, s_refs[1][i, :, TK:], NEG_INF)
        m = m_ref[i, :, TK:]
        m_new = jnp.maximum(m, jnp.max(st, axis=0, keepdims=True))
        p = jnp.exp2(st - m_new).astype(BF16)
        acc_ref[i, :, TK:] = acc_ref[i, :, TK:] * jnp.exp2(m - m_new) + _dot(v_refs[i][last], p)

    for i in range(n):
        first_diagonal(i)
    for i in range(n):
        second_diagonal(i)


ACC_SCRATCH = 5


def _sweep_scratch(n):
    return ([pltpu.VMEM((n, TK, TQ), F32)] * 2 + [pltpu.VMEM((n, 1, TQ), F32)] * 2
            + [pltpu.VMEM((n, 1, TQ), F32), pltpu.VMEM((n, DV_AUG, TQ), F32),
               pltpu.VMEM((n, LANES, TQ), BF16)])


def _transposed(q):
    return jnp.transpose(q)


def _normalised(acc_ref, i):
    return acc_ref[i, :HEAD_DIM, :] / acc_ref[i, HEAD_DIM:HEAD_DIM + 1, :]


def _fox_kernel(q_ref, k_ref, v_ref, o_ref, *scratch, hp):
    qi = pl.program_id(2)
    qts = [functools.partial(lambda hh: _transposed(q_ref[hh]), hh) for hh in range(hp)]
    k_refs = [k_ref.at[hh] for hh in range(hp)]
    v_refs = [v_ref.at[:, pl.ds(hh * DV_AUG, DV_AUG), :] for hh in range(hp)]
    _flash_sweep(qts, k_refs, v_refs, qi, scratch)
    for hh in range(hp):
        o_ref[pl.ds(hh * HEAD_DIM, HEAD_DIM), :] = _normalised(scratch[ACC_SCRATCH], hh).astype(o_ref.dtype)


def _fox_attention(qa, ka, vft, *, hp=6):
    b, heads, s, _ = qa.shape
    nkv = vft.shape[1]
    return pl.pallas_call(
        functools.partial(_fox_kernel, hp=hp),
        grid=(b, heads // hp, s // TQ),
        in_specs=[
            pl.BlockSpec((None, hp, TQ, LANES), lambda bi, hi, qi: (bi, hi, qi, 0)),
            pl.BlockSpec((None, hp, s, LANES), lambda bi, hi, qi: (bi, hi, 0, 0)),
            pl.BlockSpec((None, nkv, hp * DV_AUG, TK), lambda bi, hi, qi: (bi, 0, hi, 0)),
        ],
        out_specs=pl.BlockSpec((None, hp * HEAD_DIM, TQ), lambda bi, hi, qi: (bi, hi, qi)),
        out_shape=jax.ShapeDtypeStruct((b, heads * HEAD_DIM, s), BF16),
        scratch_shapes=_sweep_scratch(hp),
        compiler_params=pltpu.CompilerParams(
            dimension_semantics=("parallel", "parallel", "arbitrary"),
            vmem_limit_bytes=VMEM_LIMIT),
        name="fox_attention",
    )(qa, ka, vft)


def _diff_kernel(q_ref, k_ref, v_ref, lq1_ref, lk1_ref, lq2_ref, lk2_ref, g_ref, o_ref,
                 *scratch, hp, lam_init):
    qi = pl.program_id(2)
    feature = lax.broadcasted_iota(jnp.int32, (LANES, TQ), 0)
    first_map = feature < 2 * DIFF_QK_DIM
    qts, k_refs, v_refs = [], [], []

    transposed = {}

    def one_map(hh, first):
        if hh not in transposed:
            transposed[hh] = _transposed(q_ref[hh])
        qt = transposed[hh]
        zero = jnp.zeros_like(qt)
        return jnp.where(first_map, qt, zero) if first else jnp.where(first_map, zero, qt)

    for hh in range(hp):
        qts += [functools.partial(one_map, hh, True), functools.partial(one_map, hh, False)]
        k_refs += [k_ref.at[hh]] * 2
        v_refs += [v_ref.at[:, pl.ds(hh * DV_AUG, DV_AUG), :]] * 2
    _flash_sweep(qts, k_refs, v_refs, qi, scratch)

    lam = (jnp.exp(jnp.sum(lq1_ref[...] * lk1_ref[...], axis=-1, keepdims=True))
           - jnp.exp(jnp.sum(lq2_ref[...] * lk2_ref[...], axis=-1, keepdims=True))
           + lam_init)
    for hh in range(hp):
        o = _normalised(scratch[ACC_SCRATCH], 2 * hh) - lam * _normalised(scratch[ACC_SCRATCH], 2 * hh + 1)
        ms = jnp.mean(o * o, axis=0, keepdims=True)
        o = o * lax.rsqrt(ms + LN_EPS)
        o = o * g_ref[hh] * (1.0 - lam_init)
        o_ref[pl.ds(hh * DIFF_V_DIM, DIFF_V_DIM), :] = o.astype(o_ref.dtype)


def _diff_attention(q2, k2, vdt, lq1, lk1, lq2, lk2, gcol, *, layer, lam_init, hp=4):
    b, heads, s, _ = q2.shape
    nkv = vdt.shape[1]
    vec = lambda a: pl.BlockSpec((None,) + a.shape[1:], lambda bi, hi, qi: (layer, 0, 0))
    return pl.pallas_call(
        functools.partial(_diff_kernel, hp=hp, lam_init=lam_init),
        grid=(b, heads // hp, s // TQ),
        in_specs=[
            pl.BlockSpec((None, hp, TQ, LANES), lambda bi, hi, qi: (bi, hi, qi, 0)),
            pl.BlockSpec((None, hp, s, LANES), lambda bi, hi, qi: (bi, hi, 0, 0)),
            pl.BlockSpec((None, nkv, hp * DV_AUG, TK), lambda bi, hi, qi: (bi, 0, hi, 0)),
            vec(lq1), vec(lk1), vec(lq2), vec(lk2),
            pl.BlockSpec((None, hp, DIFF_V_DIM, 1), lambda bi, hi, qi: (layer, hi, 0, 0)),
        ],
        out_specs=pl.BlockSpec((None, hp * DIFF_V_DIM, TQ), lambda bi, hi, qi: (bi, hi, qi)),
        out_shape=jax.ShapeDtypeStruct((b, heads * DIFF_V_DIM, s), BF16),
        scratch_shapes=_sweep_scratch(2 * hp),
        compiler_params=pltpu.CompilerParams(
            dimension_semantics=("parallel", "parallel", "arbitrary"),
            vmem_limit_bytes=VMEM_LIMIT),
        name="diff_attention",
    )(q2, k2, vdt, lq1, lk1, lq2, lk2, gcol)


def _fox_bias_tables(fox_heads):
    nq = fox_heads * LANES
    nterm = 3 * fox_heads
    assert FOX_EXT + 2 * nterm <= LANES
    eqk = np.zeros((3 * LANES, 2 * LANES), np.float32)
    ones = np.zeros((1, 2 * nq), np.float32)
    for h in range(fox_heads):
        for t in range(3):
            a = FOX_EXT + fox_heads * t + h
            eqk[t * LANES + GATE_LANE + h, a + nterm] = 1.0
            eqk[t * LANES + GATE_LANE + h, LANES + a] = -1.0
            ones[0, h * LANES + a] = 1.0
            ones[0, nq + h * LANES + a + nterm] = 1.0
    return jnp.asarray(eqk, BF16), jnp.asarray(ones, F32)


def _alibi_tables(seq, diff_heads):
    nd = diff_heads * LANES
    ext = np.zeros((seq, 2 * nd), np.float32)
    pos = np.arange(seq, dtype=np.float32)
    slopes = np.float32(2.0) ** (np.float32(-8.0) * np.arange(1, diff_heads + 1, dtype=np.float32)
                                 / np.float32(diff_heads))
    for h in range(diff_heads):
        q_terms = _split3_np(np.float32(LOG2E) * (-slopes[h] * pos))
        k_terms = _split3_np(np.float32(LOG2E) * (slopes[h] * pos))
        for mp in range(2):
            base = h * LANES + mp * 2 * DIFF_QK_DIM + DIFF_EXT
            for t in range(3):
                ext[:, base + t] = 1.0
                ext[:, base + 3 + t] = q_terms[t]
                ext[:, nd + base + t] = k_terms[t]
                ext[:, nd + base + 3 + t] = 1.0
    return jnp.asarray(ext, BF16)


def _pad_heads(w, heads, dim):
    lead = w.shape[:-1]
    w = w.reshape(lead + (heads, dim))
    return jnp.pad(w, ((0, 0),) * (len(lead) + 1) + ((0, LANES - dim),)).reshape(lead + (heads * LANES,))


def _pad_maps(w, heads):
    lead = w.shape[:-1]
    w = w.reshape(lead + (heads, 2, DIFF_QK_DIM))
    w = jnp.pad(w, ((0, 0),) * (len(lead) + 2) + ((0, DIFF_QK_DIM),))
    return w.reshape(lead + (heads * LANES,))


def kernel(x, ffn_a_w_gate, ffn_a_w_up, ffn_a_w_down, norm_a_g, norm_a_b, w_in, fox_f_bias, diff_lambda_q1, diff_lambda_k1, diff_lambda_q2, diff_lambda_k2, diff_norm_g, sgu_norm_g, sgu_norm_b, sgu_w_s, sgu_b_s, w_out, norm_m_g, norm_m_b, ffn_b_w_gate, ffn_b_w_up, ffn_b_w_down, norm_b_g, norm_b_b):
    bsz, seq, d = x.shape
    depth = w_in.shape[0]
    fox_heads = fox_f_bias.shape[1]
    fox_w = fox_heads * HEAD_DIM
    diff_w = diff_norm_g.shape[1]
    diff_heads = diff_w // DIFF_V_DIM
    sgu_w = sgu_norm_g.shape[1]
    alpha = (2 * depth) ** 0.25

    eqk, ones = _fox_bias_tables(fox_heads)
    ext2 = _alibi_tables(seq, diff_heads)
    tri = jnp.asarray(np.tril(np.ones((TM_IN, TM_IN), np.float32)), BF16)

    bf = lambda w: w.astype(BF16)
    rows = lambda v: v.reshape(depth, 1, -1)
    wa = (bf(ffn_a_w_gate), bf(ffn_a_w_up), bf(ffn_a_w_down), rows(norm_a_g), rows(norm_a_b))
    wb = (bf(ffn_b_w_gate), bf(ffn_b_w_up), bf(ffn_b_w_down), rows(norm_b_g), rows(norm_b_b))
    wo = (bf(w_out), rows(norm_m_g), rows(norm_m_b))

    wi = bf(w_in)
    i1 = 3 * fox_w
    i2 = i1 + fox_heads
    i3 = i2 + 3 * diff_w
    wqk = jnp.concatenate([_pad_heads(wi[..., :fox_w], fox_heads, HEAD_DIM),
                           _pad_heads(wi[..., fox_w:2 * fox_w], fox_heads, HEAD_DIM)], axis=-1)
    wfvt = jnp.swapaxes(wi[..., 2 * fox_w:i1], 1, 2)
    fb = rows(jnp.pad(fox_f_bias, ((0, 0), (GATE_LANE, LANES - GATE_LANE - fox_heads))))
    wdqk = jnp.concatenate([_pad_maps(wi[..., i2:i2 + diff_w], diff_heads),
                            _pad_maps(wi[..., i2 + diff_w:i2 + 2 * diff_w], diff_heads)], axis=-1)
    gate0 = diff_heads * LANES + GATE_LANE
    wdqk = wdqk.at[..., gate0:gate0 + fox_heads].set(wi[..., i1:i2])
    wdvt = jnp.swapaxes(wi[..., i2 + 2 * diff_w:i3], 1, 2)
    wsu, wsg = wi[..., i3:i3 + sgu_w], wi[..., i3 + sgu_w:]
    bs_exp = jnp.repeat(jnp.swapaxes(sgu_b_s, 1, 2), SGU_HEAD_DIM, axis=2)
    lam = tuple(rows(v) for v in (diff_lambda_q1, diff_lambda_k1, diff_lambda_q2, diff_lambda_k2))
    gcol = diff_norm_g.reshape(depth, diff_heads, DIFF_V_DIM, 1)

    h = x
    for l in range(depth):
        lam_init = 0.8 - 0.6 * math.exp(-0.3 * l)
        h = _ffn_ln(h.reshape(bsz * seq, d), *wa, layer=l, alpha=alpha).reshape(bsz, seq, d)
        qa, ka, vft, q2, k2, vdt, ys = _mixer_in(
            h, wqk, wfvt, fb, wdqk, wdvt, tri, eqk, ones, ext2,
            wsu, wsg, rows(sgu_norm_g), rows(sgu_norm_b), sgu_w_s, bs_exp,
            layer=l, fox_heads=fox_heads, diff_heads=diff_heads)
        yft = _fox_attention(qa, ka, vft)
        ydt = _diff_attention(q2, k2, vdt, *lam, gcol, layer=l, lam_init=lam_init)
        h = _out_ffn(yft, ydt, ys, h, *wo, *wb, layer=l, alpha=alpha)
    return h
```
